```python
import jax, jax.numpy as jnp
from jax import lax
import numpy as np

D_MODEL = 1024
BATCH = 8
SEQ = 2048
DEPTH = 2
DEC_BATCH = 128
DEC_SEQ = 4
PAST_LEN = 16384
PAGE_SIZE = 128

D_LRU = D_MODEL // 2
LRU_BLOCKS = 8
LRU_BW = D_LRU // LRU_BLOCKS
CONV_W = 4
LRU_C = 8.0
D_RWKV = D_MODEL - D_LRU
RWKV_HEAD = 64
RWKV_HEADS = D_RWKV // RWKV_HEAD
R_DECAY = 64
R_ICLR = 64
R_GATE = 128
D_MIX = D_LRU + D_RWKV
RWKV_PROJ_W = 3 * D_RWKV + R_DECAY + R_ICLR + R_GATE
IN_W = 2 * D_LRU + RWKV_PROJ_W
D_FF = 256 * ((8 * D_MODEL // 3 + 255) // 256)
N_MOD = 9
NORM_EPS = 1e-6
GN_EPS = 64e-5
L2_EPS = 1e-12

kernel_name = "hybrid_rglru_rwkv7_adaln_step"


def _rms(x, g):
    xf = x.astype(jnp.float32)
    y = xf * lax.rsqrt(jnp.mean(xf * xf, axis=-1, keepdims=True) + NORM_EPS)
    return (y * g.astype(jnp.float32)).astype(x.dtype)


def _modulate(h, shift, scale):
    return h * (1 + scale[:, None, :]) + shift[:, None, :]


def _swiglu(h, w_gate, w_up, w_down):
    return (jax.nn.silu(h @ w_gate) * (h @ w_up)) @ w_down


def _linear_scan(a, b, h0):
    b = b.at[:, 0].add(a[:, 0] * h0)

    def combine(left, right):
        a_l, b_l = left
        a_r, b_r = right
        return a_l * a_r, a_r * b_l + b_r

    _, h = lax.associative_scan(combine, (a, b), axis=1)
    return h


def _wkv7(S0, r, w, k, v, a, b):
    def step(S, inp):
        r_t, w_t, k_t, v_t, a_t, b_t = inp
        sa = jnp.einsum("bhvk,bhk->bhv", S, a_t)
        S = S * w_t[:, :, None, :] + sa[..., None] * b_t[:, :, None, :] + v_t[..., None] * k_t[:, :, None, :]
        return S, jnp.einsum("bhvk,bhk->bhv", S, r_t)

    xs = tuple(jnp.moveaxis(t, 1, 0) for t in (r, w, k, v, a, b))
    S, ys = lax.scan(step, S0, xs)
    return jnp.moveaxis(ys, 0, 1), S


def _rglru_group(xb, gb, conv_buf, h0, p):
    B, T, _ = xb.shape
    f32 = jnp.float32
    xpad = jnp.concatenate([conv_buf.astype(xb.dtype), xb], axis=1)
    cw = p["lru_conv_w"]
    xc = xpad[:, CONV_W - 1:] * cw[CONV_W - 1] + p["lru_conv_b"]
    for j in range(CONV_W - 1):
        xc = xc + xpad[:, j:j + T] * cw[j]
    xcb = xc.reshape(B, T, LRU_BLOCKS, LRU_BW)
    gate_x = jax.nn.sigmoid((jnp.einsum("btnc,ncd->btnd", xcb, p["lru_wx"]).reshape(B, T, D_LRU) + p["lru_bx"]).astype(f32))
    gate_a = jax.nn.sigmoid((jnp.einsum("btnc,ncd->btnd", xcb, p["lru_wa"]).reshape(B, T, D_LRU) + p["lru_ba"]).astype(f32))
    log_a = -LRU_C * gate_a * jax.nn.softplus(-p["lru_lambda"].astype(f32))
    a = jnp.exp(log_a)
    b = jnp.sqrt(-jnp.expm1(2.0 * log_a)) * gate_x * xc.astype(f32)
    h = _linear_scan(a, b, h0.astype(f32))
    y = (h * jax.nn.gelu(gb.astype(f32))).astype(xb.dtype)
    return y, xpad[:, T:], h[:, -1]


def _rwkv7_group(pr, shift_buf, S0, p):
    B, T, _ = pr.shape
    f32 = jnp.float32
    prev = jnp.concatenate([shift_buf[:, None, :].astype(pr.dtype), pr[:, :-1]], axis=1)
    xs = pr + (prev - pr) * p["rwkv_mu"]
    o1, o2, o3 = D_RWKV, 2 * D_RWKV, 3 * D_RWKV
    o4, o5 = o3 + R_DECAY, o3 + R_DECAY + R_ICLR
    r, k, v = xs[..., :o1], xs[..., o1:o2], xs[..., o2:o3]
    wd, ad, gd = xs[..., o3:o4], xs[..., o4:o5], xs[..., o5:]
    w_log = -jax.nn.softplus(-(p["rwkv_w0"] + jnp.tanh(wd) @ p["rwkv_w2"]).astype(f32)) - 0.5
    decay = jnp.exp(-jnp.exp(w_log))
    a_rate = jax.nn.sigmoid((p["rwkv_a0"] + ad @ p["rwkv_a2"]).astype(f32))
    g = (jax.nn.sigmoid(gd) @ p["rwkv_g2"]).astype(f32)

    def heads(t):
        return t.reshape(B, T, RWKV_HEADS, RWKV_HEAD)

    kf = k.astype(f32)
    kk = heads(kf * p["rwkv_k_k"])
    kk = kk / jnp.maximum(jnp.sqrt(jnp.sum(kk * kk, axis=-1, keepdims=True)), L2_EPS)
    kf = kf * (1.0 + (a_rate - 1.0) * p["rwkv_k_a"])
    rh, kh, vh = heads(r.astype(f32)), heads(kf), heads(v.astype(f32))
    y, S = _wkv7(S0.astype(f32), rh, heads(decay), kh, vh, -kk, kk * heads(a_rate))
    mu = jnp.mean(y, axis=-1, keepdims=True)
    var = jnp.mean(jnp.square(y - mu), axis=-1, keepdims=True)
    yn = ((y - mu) * lax.rsqrt(var + GN_EPS)).reshape(B, T, D_RWKV) * p["rwkv_ln_w"] + p["rwkv_ln_b"]
    bonus = jnp.sum(rh * kh * p["rwkv_r_k"], axis=-1, keepdims=True) * vh
    out = (yn + bonus.reshape(B, T, D_RWKV)) * g
    return out.astype(pr.dtype), pr[:, -1], S


def _mixer(h, conv_buf, h0, shift_buf, S0, p):
    proj = h @ p["w_in"]
    y_lru, n_conv, n_h = _rglru_group(proj[..., :D_LRU], proj[..., D_LRU:2 * D_LRU], conv_buf, h0, p)
    y_rw, n_shift, n_S = _rwkv7_group(proj[..., 2 * D_LRU:], shift_buf, S0, p)
    out = jnp.concatenate([y_lru, y_rw], axis=-1) @ p["w_out"]
    return out, (n_conv, n_h, n_shift, n_S)


def _trunk(x, c, conv, hst, shift, wkv, params, final_norm):
    cs = jax.nn.silu(c)
    outs = ([], [], [], [])
    for l in range(DEPTH):
        p = {name: arr[l] for name, arr in params.items()}
        mod = cs @ p["w_ada"] + p["b_ada"]
        sh1, sc1, g1, sh2, sc2, g2, sh3, sc3, g3 = jnp.split(mod, N_MOD, axis=-1)
        h = _modulate(_rms(x, p["ffn1_norm"]), sh1, sc1)
        x = x + 0.5 * g1[:, None, :] * _swiglu(h, p["ffn1_w_gate"], p["ffn1_w_up"], p["ffn1_w_down"])
        h = _modulate(_rms(x, p["mix_norm"]), sh2, sc2)
        m, new = _mixer(h, conv[l], hst[l], shift[l], wkv[l], p)
        x = x + g2[:, None, :] * m
        h = _modulate(_rms(x, p["ffn2_norm"]), sh3, sc3)
        x = x + 0.5 * g3[:, None, :] * _swiglu(h, p["ffn2_w_gate"], p["ffn2_w_up"], p["ffn2_w_down"])
        for lst, s in zip(outs, new):
            lst.append(s.astype(x.dtype))
    y = _rms(x, final_norm)
    return y, tuple(jnp.stack(lst) for lst in outs)


def setup_inputs(seed: int = 0) -> dict:
    key = jax.random.key(seed)
    ks = list(jax.random.split(key, 48))
    f32 = jnp.float32

    def nrm(i, shape, s):
        return s * jax.random.normal(ks[i], shape, f32)

    d = D_MODEL
    lam_u = jax.random.uniform(ks[40], (DEPTH, D_LRU), f32, 0.9, 0.999)
    lam_s = lam_u ** (1.0 / LRU_C)
    return {
        "x_prompt": nrm(0, (BATCH, SEQ, d), 1.0),
        "x_sample": nrm(1, (DEC_BATCH, DEC_SEQ, d), 1.0),
        "c_prompt": nrm(2, (BATCH, d), 1.0),
        "c_sample": nrm(3, (DEC_BATCH, d), 1.0),
        "state_lru_conv": nrm(4, (DEPTH, DEC_BATCH, CONV_W - 1, D_LRU), 1.0),
        "state_lru_h": nrm(5, (DEPTH, DEC_BATCH, D_LRU), 0.5),
        "state_rwkv_shift": nrm(6, (DEPTH, DEC_BATCH, RWKV_PROJ_W), 1.0),
        "state_rwkv_wkv": nrm(7, (DEPTH, DEC_BATCH, RWKV_HEADS, RWKV_HEAD, RWKV_HEAD), 0.3),
        "w_ada": nrm(8, (DEPTH, d, N_MOD * d), 0.5 * d ** -0.5),
        "b_ada": nrm(9, (DEPTH, N_MOD * d), 0.02),
        "ffn1_norm": 1.0 + nrm(10, (DEPTH, d), 0.05),
        "ffn1_w_gate": nrm(11, (DEPTH, d, D_FF), d ** -0.5),
        "ffn1_w_up": nrm(12, (DEPTH, d, D_FF), d ** -0.5),
        "ffn1_w_down": nrm(13, (DEPTH, D_FF, d), D_FF ** -0.5),
        "mix_norm": 1.0 + nrm(14, (DEPTH, d), 0.05),
        "w_in": nrm(15, (DEPTH, d, IN_W), d ** -0.5),
        "w_out": nrm(16, (DEPTH, D_MIX, d), D_MIX ** -0.5),
        "lru_conv_w": nrm(17, (DEPTH, CONV_W, D_LRU), CONV_W ** -0.5),
        "lru_conv_b": nrm(18, (DEPTH, D_LRU), 0.02),
        "lru_wx": nrm(19, (DEPTH, LRU_BLOCKS, LRU_BW, LRU_BW), LRU_BW ** -0.5),
        "lru_bx": nrm(20, (DEPTH, D_LRU), 0.02),
        "lru_wa": nrm(21, (DEPTH, LRU_BLOCKS, LRU_BW, LRU_BW), LRU_BW ** -0.5),
        "lru_ba": nrm(22, (DEPTH, D_LRU), 0.02),
        "lru_lambda": jnp.log(lam_s) - jnp.log1p(-lam_s),
        "rwkv_mu": jax.random.uniform(ks[23], (DEPTH, RWKV_PROJ_W), f32, 0.0, 1.0),
        "rwkv_w0": jax.random.uniform(ks[24], (DEPTH, D_RWKV), f32, -6.0, -1.0),
        "rwkv_w2": nrm(25, (DEPTH, R_DECAY, D_RWKV), 0.5 * R_DECAY ** -0.5),
        "rwkv_a0": nrm(26, (DEPTH, D_RWKV), 0.5),
        "rwkv_a2": nrm(27, (DEPTH, R_ICLR, D_RWKV), 0.5 * R_ICLR ** -0.5),
        "rwkv_g2": nrm(28, (DEPTH, R_GATE, D_RWKV), R_GATE ** -0.5),
        "rwkv_k_k": 0.85 + nrm(29, (DEPTH, D_RWKV), 0.05),
        "rwkv_k_a": 1.0 + nrm(30, (DEPTH, D_RWKV), 0.05),
        "rwkv_r_k": nrm(31, (DEPTH, RWKV_HEADS, RWKV_HEAD), 0.1),
        "rwkv_ln_w": 1.0 + nrm(32, (DEPTH, D_RWKV), 0.05),
        "rwkv_ln_b": nrm(33, (DEPTH, D_RWKV), 0.02),
        "ffn2_norm": 1.0 + nrm(34, (DEPTH, d), 0.05),
        "ffn2_w_gate": nrm(35, (DEPTH, d, D_FF), d ** -0.5),
        "ffn2_w_up": nrm(36, (DEPTH, d, D_FF), d ** -0.5),
        "ffn2_w_down": nrm(37, (DEPTH, D_FF, d), D_FF ** -0.5),
        "final_norm": 1.0 + nrm(38, (d,), 0.05),
    }


def reference(x_prompt, x_sample, c_prompt, c_sample, state_lru_conv, state_lru_h, state_rwkv_shift, state_rwkv_wkv,
              w_ada, b_ada, ffn1_norm, ffn1_w_gate, ffn1_w_up, ffn1_w_down, mix_norm, w_in, w_out,
              lru_conv_w, lru_conv_b, lru_wx, lru_bx, lru_wa, lru_ba, lru_lambda,
              rwkv_mu, rwkv_w0, rwkv_w2, rwkv_a0, rwkv_a2, rwkv_g2, rwkv_k_k, rwkv_k_a, rwkv_r_k, rwkv_ln_w, rwkv_ln_b,
              ffn2_norm, ffn2_w_gate, ffn2_w_up, ffn2_w_down, final_norm):
    params = dict(
        w_ada=w_ada, b_ada=b_ada, ffn1_norm=ffn1_norm, ffn1_w_gate=ffn1_w_gate, ffn1_w_up=ffn1_w_up,
        ffn1_w_down=ffn1_w_down, mix_norm=mix_norm, w_in=w_in, w_out=w_out,
        lru_conv_w=lru_conv_w, lru_conv_b=lru_conv_b, lru_wx=lru_wx, lru_bx=lru_bx, lru_wa=lru_wa,
        lru_ba=lru_ba, lru_lambda=lru_lambda, rwkv_mu=rwkv_mu, rwkv_w0=rwkv_w0, rwkv_w2=rwkv_w2,
        rwkv_a0=rwkv_a0, rwkv_a2=rwkv_a2, rwkv_g2=rwkv_g2, rwkv_k_k=rwkv_k_k, rwkv_k_a=rwkv_k_a,
        rwkv_r_k=rwkv_r_k, rwkv_ln_w=rwkv_ln_w, rwkv_ln_b=rwkv_ln_b, ffn2_norm=ffn2_norm,
        ffn2_w_gate=ffn2_w_gate, ffn2_w_up=ffn2_w_up, ffn2_w_down=ffn2_w_down)
    B = x_prompt.shape[0]
    dt = x_prompt.dtype
    z_conv = jnp.zeros((DEPTH, B, CONV_W - 1, D_LRU), dt)
    z_h = jnp.zeros((DEPTH, B, D_LRU), dt)
    z_shift = jnp.zeros((DEPTH, B, RWKV_PROJ_W), dt)
    z_wkv = jnp.zeros((DEPTH, B, RWKV_HEADS, RWKV_HEAD, RWKV_HEAD), dt)
    y_prompt, p_state = _trunk(x_prompt, c_prompt, z_conv, z_h, z_shift, z_wkv, params, final_norm)
    y_sample, s_state = _trunk(x_sample, c_sample, state_lru_conv, state_lru_h, state_rwkv_shift, state_rwkv_wkv,
                               params, final_norm)
    p_conv, p_h, p_shift, p_wkv = p_state
    s_conv, s_h, s_shift, s_wkv = s_state
    return (y_prompt, y_sample, p_conv, p_h, p_shift, p_wkv, s_conv, s_h, s_shift, s_wkv)
```

```python
import functools

import jax
import jax.numpy as jnp
from jax import lax
from jax.experimental import pallas as pl
from jax.experimental.pallas import tpu as pltpu

F32 = jnp.float32
BF16 = jnp.bfloat16

D_MODEL = 1024
DEPTH = 2
D_LRU = 512
LRU_BLOCKS = 8
CONV_W = 4
LRU_C = 8.0
D_RWKV = 512
HEAD = 64
HEADS = D_RWKV // HEAD
R_DECAY = 64
R_ICLR = 64
R_GATE = 128
RWKV_PROJ_W = 3 * D_RWKV + R_DECAY + R_ICLR + R_GATE
IN_W = 2 * D_LRU + RWKV_PROJ_W
D_FF = 2816
N_MOD = 9
NORM_EPS = 1e-6
GN_EPS = 64e-5
L2_EPS = 1e-12

LANES = 128
V7X_VMEM_LIMIT = 56 * 1024 * 1024
FF_CHUNK = 704
TOKEN_TILE = 512
MIX_ROWS = 256
WKV_TIME_TILE = 32


def _const_spec(arr):
    nd = arr.ndim
    return pl.BlockSpec(arr.shape, lambda *_: (0,) * nd, pipeline_mode=pl.Buffered(1))


def _softplus(z):
    return jnp.maximum(z, 0.0) + jnp.log1p(jnp.exp(-jnp.abs(z)))


def _segsum(q, ones_ref):
    hi = q.astype(BF16)
    lo = (q - hi.astype(F32)).astype(BF16)
    ones = ones_ref[...]
    return (jnp.dot(hi, ones, preferred_element_type=F32)
            + jnp.dot(lo, ones, preferred_element_type=F32))


def _rms_mod(x, nw, sc, sh):
    ms = jnp.mean(x * x, axis=-1, keepdims=True)
    h = x * lax.rsqrt(ms + NORM_EPS) * nw
    return h * (1.0 + sc) + sh


def _ada_kernel(c_ref, w_ref, b_ref, o_ref):
    c = c_ref[...]
    cs = (c * jax.nn.sigmoid(c)).astype(BF16)
    o_ref[...] = jnp.dot(cs, w_ref[...].astype(BF16), preferred_element_type=F32) + b_ref[...]


def _ada(c_all, w_ada, b_ada):
    n = c_all.shape[0]
    nj = (N_MOD * D_MODEL) // D_MODEL
    return pl.pallas_call(
        _ada_kernel,
        grid=(DEPTH, nj),
        in_specs=[
            pl.BlockSpec((n, D_MODEL), lambda l, j: (0, 0)),
            pl.BlockSpec((None, D_MODEL, D_MODEL), lambda l, j: (l, 0, j)),
            pl.BlockSpec((None, 1, D_MODEL), lambda l, j: (l, 0, j)),
        ],
        out_specs=pl.BlockSpec((None, n, D_MODEL), lambda l, j: (l, 0, j)),
        out_shape=jax.ShapeDtypeStruct((DEPTH, n, N_MOD * D_MODEL), F32),
        compiler_params=pltpu.CompilerParams(dimension_semantics=("parallel", "parallel")),
        name="ada_mod",
    )(c_all, w_ada, b_ada.reshape(DEPTH, 1, N_MOD * D_MODEL))


class _Rows:
    def __init__(self, n_rows, tile, mod, per_row, tiles_per_batch=1):
        self.n_rows, self.tile, self.mod, self.per_row, self.tpb = n_rows, tile, mod, per_row, tiles_per_batch
        self.grid = (n_rows // tile,)

    def mod_spec(self, col):
        if self.per_row:
            return pl.BlockSpec((self.tile, D_MODEL), lambda i: (i, col))
        tpb = self.tpb
        return pl.BlockSpec((None, 1, D_MODEL), lambda i: (i // tpb, 0, col))

    def row_spec(self, width):
        return pl.BlockSpec((self.tile, width), lambda i: (i, 0))

    def tm_spec(self, width):
        if self.per_row:
            return pl.BlockSpec((self.tile, width), lambda i: (i, 0))
        tpb = self.tpb
        return pl.BlockSpec((self.tile, width), lambda i: (i % tpb, i // tpb))


def _ffn_kernel(x_ref, sh_ref, sc_ref, g_ref, nw_ref, wg_ref, wu_ref, wd_ref, *rest, final):
    if final:
        fn_ref, o_ref = rest
    else:
        (o_ref,) = rest
    x = x_ref[...]
    hb = _rms_mod(x, nw_ref[...], sc_ref[...], sh_ref[...]).astype(BF16)
    acc = jnp.zeros(x.shape, F32)
    for c in range(D_FF // FF_CHUNK):
        lo, hi = c * FF_CHUNK, (c + 1) * FF_CHUNK
        g = jnp.dot(hb, wg_ref[:, lo:hi], preferred_element_type=F32)
        u = jnp.dot(hb, wu_ref[:, lo:hi], preferred_element_type=F32)
        a = (g * jax.nn.sigmoid(g) * u).astype(BF16)
        acc = acc + jnp.dot(a, wd_ref[lo:hi, :], preferred_element_type=F32)
    y = x + 0.5 * g_ref[...] * acc
    if final:
        ms = jnp.mean(y * y, axis=-1, keepdims=True)
        y = y * lax.rsqrt(ms + NORM_EPS) * fn_ref[...]
    o_ref[...] = y


def _ffn(x, rows, first_col, nw, wg, wu, wd, final_norm=None):
    final = final_norm is not None
    ins = [x, rows.mod, rows.mod, rows.mod, nw, wg, wu, wd]
    specs = [rows.row_spec(D_MODEL), rows.mod_spec(first_col), rows.mod_spec(first_col + 1),
             rows.mod_spec(first_col + 2), _const_spec(nw), _const_spec(wg), _const_spec(wu), _const_spec(wd)]
    if final:
        ins.append(final_norm)
        specs.append(_const_spec(final_norm))
    return pl.pallas_call(
        functools.partial(_ffn_kernel, final=final),
        grid=rows.grid,
        in_specs=specs,
        out_specs=rows.row_spec(D_MODEL),
        out_shape=jax.ShapeDtypeStruct(x.shape, F32),
        compiler_params=pltpu.CompilerParams(dimension_semantics=("parallel",),
                                             vmem_limit_bytes=V7X_VMEM_LIMIT),
        name="ffn_final" if final else "ffn",
    )(*ins)


def _inproj_kernel(x_ref, sh_ref, sc_ref, nw_ref, w_ref, o_ref):
    hb = _rms_mod(x_ref[...], nw_ref[...], sc_ref[...], sh_ref[...]).astype(BF16)
    o_ref[...] = jnp.dot(hb, w_ref[...], preferred_element_type=F32)


def _inproj(x, rows, nw, w_in, out_tm_shape):
    return pl.pallas_call(
        _inproj_kernel,
        grid=rows.grid,
        in_specs=[rows.row_spec(D_MODEL), rows.mod_spec(3), rows.mod_spec(4), _const_spec(nw), _const_spec(w_in)],
        out_specs=rows.tm_spec(IN_W),
        out_shape=jax.ShapeDtypeStruct(out_tm_shape, F32),
        compiler_params=pltpu.CompilerParams(dimension_semantics=("parallel",),
                                             vmem_limit_bytes=V7X_VMEM_LIMIT),
        name="mix_inproj",
    )(x, rows.mod, rows.mod, nw, w_in)


def _mix1_kernel(p_ref, conv0_ref, h0_ref, shift0_ref,
                 cw_ref, cb_ref, wx_ref, bx_ref, wa_ref, ba_ref, lam_ref,
                 mu_ref, w0_ref, w2_ref, a0_ref, a2_ref, g2_ref, kk_ref, ka_ref, rk_ref, ones_ref,
                 ylru_ref, r_ref, w_ref, k_ref, v_ref, na_ref, nb_ref, g_ref, bonus_ref,
                 convn_ref, hn_ref, shiftn_ref,
                 xe_ref, pe_ref, hc_ref, a_s, b_s, h_s, *, nb_rows, steps):
    B, R = nb_rows, nb_rows * steps
    CB = (CONV_W - 1) * B

    @pl.when(pl.program_id(0) == 0)
    def _():
        xe_ref[0:CB, :] = conv0_ref[...]
        pe_ref[0:B, :] = shift0_ref[...]
        hc_ref[...] = h0_ref[...]

    xe_ref[CB:CB + R, :] = p_ref[:, 0:D_LRU]
    gb = p_ref[:, D_LRU:2 * D_LRU]
    xc = cb_ref[...] + xe_ref[CB:CB + R, :] * cw_ref[CONV_W - 1:CONV_W, :]
    for j in range(CONV_W - 1):
        xc = xc + xe_ref[j * B:j * B + R, :] * cw_ref[j:j + 1, :]
    xcb = xc.astype(BF16)
    gate_x = jax.nn.sigmoid(jnp.dot(xcb, wx_ref[...], preferred_element_type=F32) + bx_ref[...])
    gate_a = jax.nn.sigmoid(jnp.dot(xcb, wa_ref[...], preferred_element_type=F32) + ba_ref[...])
    log_a = -LRU_C * gate_a * _softplus(-lam_ref[...])
    a = jnp.exp(log_a)
    a_s[...] = a
    b_s[...] = jnp.sqrt(-jnp.tanh(log_a) * (a * a + 1.0)) * gate_x * xc

    def scan_step(t, carry):
        sl = pl.ds(pl.multiple_of(t * B, B), B)
        h = a_s[sl, :] * hc_ref[...] + b_s[sl, :]
        hc_ref[...] = h
        h_s[sl, :] = h
        return carry

    lax.fori_loop(0, steps, scan_step, 0)
    ylru_ref[...] = h_s[...] * jax.nn.gelu(gb)
    tail = xe_ref[R:R + CB, :]
    xe_ref[0:CB, :] = tail
    convn_ref[...] = tail
    hn_ref[...] = hc_ref[...]

    pr = p_ref[:, 2 * D_LRU:IN_W]
    pe_ref[B:B + R, :] = pr
    xs = pr + (pe_ref[0:R, :] - pr) * mu_ref[...]
    last = pe_ref[R:R + B, :]
    pe_ref[0:B, :] = last
    shiftn_ref[...] = last

    r = xs[:, 0:D_RWKV]
    k = xs[:, D_RWKV:2 * D_RWKV]
    v = xs[:, 2 * D_RWKV:3 * D_RWKV]
    lowrank = xs[:, 3 * D_RWKV:3 * D_RWKV + R_DECAY + R_ICLR]
    gd = xs[:, 3 * D_RWKV + R_DECAY + R_ICLR:RWKV_PROJ_W]
    wlin = w0_ref[...] + jnp.dot(jnp.tanh(lowrank).astype(BF16), w2_ref[...], preferred_element_type=F32)
    w_log = -_softplus(-wlin) - 0.5
    a_rate = jax.nn.sigmoid(a0_ref[...] + jnp.dot(lowrank.astype(BF16), a2_ref[...], preferred_element_type=F32))
    kkr = k * kk_ref[...]
    kkn = kkr / jnp.maximum(jnp.sqrt(_segsum(kkr * kkr, ones_ref)), L2_EPS)
    kf = k * (1.0 + (a_rate - 1.0) * ka_ref[...])
    r_ref[...] = r
    w_ref[...] = jnp.exp(-jnp.exp(w_log))
    k_ref[...] = kf
    v_ref[...] = v
    na_ref[...] = -kkn
    nb_ref[...] = kkn * a_rate
    g_ref[...] = jnp.dot(jax.nn.sigmoid(gd).astype(BF16), g2_ref[...], preferred_element_type=F32)
    bonus_ref[...] = _segsum(r * kf * rk_ref[...], ones_ref) * v


def _mix1(proj, conv0, h0, shift0, params, nb_rows, n_steps):
    n = proj.shape[0]
    steps = MIX_ROWS // nb_rows
    R = steps * nb_rows
    CB = (CONV_W - 1) * nb_rows
    row = lambda w: pl.BlockSpec((R, w), lambda i: (i, 0))
    full = lambda a: pl.BlockSpec(a.shape, lambda i: (0,) * a.ndim)
    big = jax.ShapeDtypeStruct((n, D_RWKV), F32)
    outs = pl.pallas_call(
        functools.partial(_mix1_kernel, nb_rows=nb_rows, steps=steps),
        grid=(n_steps // steps,),
        in_specs=[row(IN_W), full(conv0), full(h0), full(shift0)] + [_const_spec(p) for p in params],
        out_specs=[row(D_RWKV)] * 9 + [full(conv0), full(h0), full(shift0)],
        out_shape=[big] * 9 + [jax.ShapeDtypeStruct(conv0.shape, F32), jax.ShapeDtypeStruct(h0.shape, F32),
                               jax.ShapeDtypeStruct(shift0.shape, F32)],
        scratch_shapes=[
            pltpu.VMEM((CB + R, D_LRU), F32),
            pltpu.VMEM((nb_rows + R, RWKV_PROJ_W), F32),
            pltpu.VMEM((nb_rows, D_LRU), F32),
            pltpu.VMEM((R, D_LRU), F32),
            pltpu.VMEM((R, D_LRU), F32),
            pltpu.VMEM((R, D_LRU), F32),
        ],
        compiler_params=pltpu.CompilerParams(dimension_semantics=("arbitrary",),
                                             vmem_limit_bytes=V7X_VMEM_LIMIT),
        name="mix_pre",
    )(proj, conv0, h0, shift0, *params)
    return outs


def _wkv_kernel(r_ref, w_ref, k_ref, a_ref, b_ref, v_ref, s0_ref, y_ref, s_ref, *, nv, steps):
    @pl.when(pl.program_id(1) == 0)
    def _():
        s_ref[...] = s0_ref[...]

    def step(t, carry):
        rt, wt, kt, at, bt = r_ref[t], w_ref[t], k_ref[t], a_ref[t], b_ref[t]
        vt = v_ref[t]
        for p0 in range(0, nv, 8):
            rows = []
            for p in range(p0, p0 + 8):
                s = s_ref[p]
                sa = jnp.sum(s * at, axis=0, keepdims=True)
                s = s * wt + sa * bt + vt[p:p + 1, :] * kt
                s_ref[p] = s
                rows.append(jnp.sum(s * rt, axis=0, keepdims=True))
            y_ref[t, p0:p0 + 8, :] = jnp.concatenate(rows, axis=0)
        return carry

    lax.fori_loop(0, steps, step, 0)


def _wkv(ops_t, v_t, s0_t, nv, n_steps, time_tile):
    lanes = s0_t.shape[-1]
    op_spec = pl.BlockSpec((time_tile, HEAD, LANES), lambda g, ti: (ti, 0, g))
    v_spec = pl.BlockSpec((time_tile, nv, LANES), lambda g, ti: (ti, 0, g))
    s_spec = pl.BlockSpec((nv, HEAD, LANES), lambda g, ti: (0, 0, g))
    return pl.pallas_call(
        functools.partial(_wkv_kernel, nv=nv, steps=time_tile),
        grid=(lanes // LANES, n_steps // time_tile),
        in_specs=[op_spec] * 5 + [v_spec, s_spec],
        out_specs=[v_spec, s_spec],
        out_shape=[jax.ShapeDtypeStruct((n_steps, nv, lanes), F32), jax.ShapeDtypeStruct(s0_t.shape, F32)],
        compiler_params=pltpu.CompilerParams(dimension_semantics=("parallel", "arbitrary"),
                                             vmem_limit_bytes=V7X_VMEM_LIMIT),
        name="wkv_scan",
    )(*ops_t, v_t, s0_t)


def _to_wkv_operand(x, n_steps, nb, split):
    x = x.reshape(n_steps, nb, HEADS, HEAD).transpose(0, 3, 1, 2)
    x = jnp.broadcast_to(x[:, :, None], (n_steps, HEAD, split, nb, HEADS))
    return x.reshape(n_steps, HEAD, split * nb * HEADS)


def _to_wkv_value(v, n_steps, nb, split):
    v = v.reshape(n_steps, nb, HEADS, split, HEAD // split).transpose(0, 4, 3, 1, 2)
    return v.reshape(n_steps, HEAD // split, split * nb * HEADS)


def _from_wkv_value(y, n_steps, nb, split):
    y = y.reshape(n_steps, HEAD // split, split, nb, HEADS).transpose(0, 3, 4, 2, 1)
    return y.reshape(n_steps * nb, D_RWKV)


def _to_wkv_state(s, nb, split):
    s = s.reshape(nb, HEADS, split, HEAD // split, HEAD).transpose(3, 4, 2, 0, 1)
    return s.reshape(HEAD // split, HEAD, split * nb * HEADS)


def _from_wkv_state(s, nb, split):
    s = s.reshape(HEAD // split, HEAD, split, nb, HEADS).transpose(3, 4, 2, 0, 1)
    return s.reshape(nb, HEADS, HEAD, HEAD)


def _mix2_kernel(x_ref, g2_ref, ylru_ref, y_ref, bonus_ref, g_ref, lnw_ref, lnb_ref, ones_ref, wo_ref, o_ref):
    y = y_ref[...]
    mu = _segsum(y, ones_ref) * (1.0 / HEAD)
    d = y - mu
    var = _segsum(d * d, ones_ref) * (1.0 / HEAD)
    yn = d * lax.rsqrt(var + GN_EPS) * lnw_ref[...] + lnb_ref[...]
    yrw = ((yn + bonus_ref[...]) * g_ref[...]).astype(BF16)
    m = (jnp.dot(ylru_ref[...].astype(BF16), wo_ref[0:D_LRU, :], preferred_element_type=F32)
         + jnp.dot(yrw, wo_ref[D_LRU:D_LRU + D_RWKV, :], preferred_element_type=F32))
    o_ref[...] = x_ref[...] + g2_ref[...] * m


def _mix2(x, rows, ylru, y, bonus, g, lnw, lnb, ones_bd, w_out):
    tm = rows.tm_spec(D_RWKV)
    return pl.pallas_call(
        _mix2_kernel,
        grid=rows.grid,
        in_specs=[rows.row_spec(D_MODEL), rows.mod_spec(5), tm, tm, tm, tm,
                  _const_spec(lnw), _const_spec(lnb), _const_spec(ones_bd), _const_spec(w_out)],
        out_specs=rows.row_spec(D_MODEL),
        out_shape=jax.ShapeDtypeStruct(x.shape, F32),
        compiler_params=pltpu.CompilerParams(dimension_semantics=("parallel",),
                                             vmem_limit_bytes=V7X_VMEM_LIMIT),
        name="mix_post",
    )(x, rows.mod, ylru, y, bonus, g, lnw, lnb, ones_bd, w_out)


def _block_diag(w):
    n, c, d = w.shape
    eye = jnp.eye(n, dtype=w.dtype)
    return (eye[:, None, :, None] * w[:, :, None, :]).reshape(n * c, n * d)


def _row(v):
    return v.reshape(1, -1)


def _layer_params(P, l):
    z = jnp.zeros((R_DECAY, D_RWKV), F32)
    w2p = jnp.concatenate([P["rwkv_w2"][l], z], axis=0).astype(BF16)
    a2p = jnp.concatenate([z, P["rwkv_a2"][l]], axis=0).astype(BF16)
    ones_bd = _block_diag(jnp.ones((HEADS, HEAD, HEAD), F32)).astype(BF16)
    mix1 = [P["lru_conv_w"][l], _row(P["lru_conv_b"][l]),
            _block_diag(P["lru_wx"][l]).astype(BF16), _row(P["lru_bx"][l]),
            _block_diag(P["lru_wa"][l]).astype(BF16), _row(P["lru_ba"][l]), _row(P["lru_lambda"][l]),
            _row(P["rwkv_mu"][l]), _row(P["rwkv_w0"][l]), w2p, _row(P["rwkv_a0"][l]), a2p,
            P["rwkv_g2"][l].astype(BF16), _row(P["rwkv_k_k"][l]), _row(P["rwkv_k_a"][l]),
            _row(P["rwkv_r_k"][l]), ones_bd]
    return dict(
        ffn1=(_row(P["ffn1_norm"][l]), P["ffn1_w_gate"][l].astype(BF16), P["ffn1_w_up"][l].astype(BF16),
              P["ffn1_w_down"][l].astype(BF16)),
        ffn2=(_row(P["ffn2_norm"][l]), P["ffn2_w_gate"][l].astype(BF16), P["ffn2_w_up"][l].astype(BF16),
              P["ffn2_w_down"][l].astype(BF16)),
        mix_norm=_row(P["mix_norm"][l]), w_in=P["w_in"][l].astype(BF16), w_out=P["w_out"][l].astype(BF16),
        mix1=mix1, ln_w=_row(P["rwkv_ln_w"][l]), ln_b=_row(P["rwkv_ln_b"][l]), ones_bd=ones_bd)


def _trunk(x, make_rows, mods, conv, hst, shift, wkv, layers, final_norm, nb, n_steps, split, time_tile):
    n_conv, n_h, n_shift, n_wkv = [], [], [], []
    nv = HEAD // split
    for l, lp in enumerate(layers):
        rows = make_rows(mods[l])
        x = _ffn(x, rows, 0, *lp["ffn1"])
        proj_shape = (n_steps * nb, IN_W) if rows.per_row else (n_steps, nb * IN_W)
        proj = _inproj(x, rows, lp["mix_norm"], lp["w_in"], proj_shape).reshape(n_steps * nb, IN_W)
        conv0 = conv[l].transpose(1, 0, 2).reshape((CONV_W - 1) * nb, D_LRU)
        (ylru, r, w, k, v, na, nbv, g, bonus, convn, hn, shiftn) = _mix1(
            proj, conv0, hst[l], shift[l], lp["mix1"], nb, n_steps)
        ops_t = [_to_wkv_operand(a, n_steps, nb, split) for a in (r, w, k, na, nbv)]
        y_t, s_t = _wkv(ops_t, _to_wkv_value(v, n_steps, nb, split), _to_wkv_state(wkv[l], nb, split),
                        nv, n_steps, time_tile)
        y = _from_wkv_value(y_t, n_steps, nb, split)
        if not rows.per_row:
            tm = lambda a: a.reshape(n_steps, nb * D_RWKV)
            ylru, y, bonus, g = tm(ylru), tm(y), tm(bonus), tm(g)
        x = _mix2(x, rows, ylru, y, bonus, g, lp["ln_w"], lp["ln_b"], lp["ones_bd"], lp["w_out"])
        x = _ffn(x, rows, 6, *lp["ffn2"], final_norm=final_norm if l == DEPTH - 1 else None)
        n_conv.append(convn.reshape(CONV_W - 1, nb, D_LRU).transpose(1, 0, 2))
        n_h.append(hn)
        n_shift.append(shiftn)
        n_wkv.append(_from_wkv_state(s_t, nb, split))
    return x, (jnp.stack(n_conv), jnp.stack(n_h), jnp.stack(n_shift), jnp.stack(n_wkv))


def kernel(x_prompt, x_sample, c_prompt, c_sample, state_lru_conv, state_lru_h, state_rwkv_shift, state_rwkv_wkv, w_ada, b_ada, ffn1_norm, ffn1_w_gate, ffn1_w_up, ffn1_w_down, mix_norm, w_in, w_out, lru_conv_w, lru_conv_b, lru_wx, lru_bx, lru_wa, lru_ba, lru_lambda, rwkv_mu, rwkv_w0, rwkv_w2, rwkv_a0, rwkv_a2, rwkv_g2, rwkv_k_k, rwkv_k_a, rwkv_r_k, rwkv_ln_w, rwkv_ln_b, ffn2_norm, ffn2_w_gate, ffn2_w_up, ffn2_w_down, final_norm):
    P = dict(ffn1_norm=ffn1_norm, ffn1_w_gate=ffn1_w_gate, ffn1_w_up=ffn1_w_up, ffn1_w_down=ffn1_w_down,
             mix_norm=mix_norm, w_in=w_in, w_out=w_out, lru_conv_w=lru_conv_w, lru_conv_b=lru_conv_b,
             lru_wx=lru_wx, lru_bx=lru_bx, lru_wa=lru_wa, lru_ba=lru_ba, lru_lambda=lru_lambda,
             rwkv_mu=rwkv_mu, rwkv_w0=rwkv_w0, rwkv_w2=rwkv_w2, rwkv_a0=rwkv_a0, rwkv_a2=rwkv_a2,
             rwkv_g2=rwkv_g2, rwkv_k_k=rwkv_k_k, rwkv_k_a=rwkv_k_a, rwkv_r_k=rwkv_r_k,
             rwkv_ln_w=rwkv_ln_w, rwkv_ln_b=rwkv_ln_b, ffn2_norm=ffn2_norm, ffn2_w_gate=ffn2_w_gate,
             ffn2_w_up=ffn2_w_up, ffn2_w_down=ffn2_w_down)
    layers = [_layer_params(P, l) for l in range(DEPTH)]
    fn = _row(final_norm)
    bp, tp, _ = x_prompt.shape
    bs, ts, _ = x_sample.shape

    mod = _ada(jnp.concatenate([c_prompt, c_sample], axis=0), w_ada, b_ada)

    tpb = tp // TOKEN_TILE
    rows_p = lambda m: _Rows(bp * tp, TOKEN_TILE, m[:bp].reshape(bp, 1, N_MOD * D_MODEL), False, tpb)
    zeros = lambda *s: jnp.zeros((DEPTH,) + s, F32)
    y_p, st_p = _trunk(x_prompt.reshape(bp * tp, D_MODEL), rows_p, mod,
                       zeros(bp, CONV_W - 1, D_LRU), zeros(bp, D_LRU), zeros(bp, RWKV_PROJ_W),
                       zeros(bp, HEADS, HEAD, HEAD), layers, fn, bp, tp, LANES // (bp * HEADS), WKV_TIME_TILE)

    rows_s = lambda m: _Rows(bs * ts, TOKEN_TILE, jnp.tile(m[bp:], (ts, 1)), True)
    x_s = x_sample.transpose(1, 0, 2).reshape(ts * bs, D_MODEL)
    y_s, st_s = _trunk(x_s, rows_s, mod, state_lru_conv, state_lru_h, state_rwkv_shift, state_rwkv_wkv,
                       layers, fn, bs, ts, 1, ts)

    y_prompt = y_p.reshape(bp, tp, D_MODEL)
    y_sample = y_s.reshape(ts, bs, D_MODEL).transpose(1, 0, 2)
    return (y_prompt, y_sample) + st_p + st_s
```

```python
import functools

import jax
import jax.numpy as jnp
from jax import lax
from jax.experimental import pallas as pl
from jax.experimental.pallas import tpu as pltpu

F32 = jnp.float32
BF16 = jnp.bfloat16

D_MODEL = 1024
DEPTH = 2
D_LRU = 512
LRU_BLOCKS = 8
CONV_W = 4
LRU_C = 8.0
D_RWKV = 512
HEAD = 64
HEADS = D_RWKV // HEAD
R_DECAY = 64
R_ICLR = 64
R_GATE = 128
RWKV_PROJ_W = 3 * D_RWKV + R_DECAY + R_ICLR + R_GATE
IN_W = 2 * D_LRU + RWKV_PROJ_W
D_FF = 2816
N_MOD = 9
NORM_EPS = 1e-6
GN_EPS = 64e-5
L2_EPS = 1e-12

LANES = 128
SUBLANES = 8
V7X_VMEM_LIMIT = 56 * 1024 * 1024
FF_CHUNK = 704
TOKEN_TILE = 512
MIX_ROWS = 256
WKV_TIME_TILE = 32

assert 2 * HEAD == LANES


def _const_spec(arr):
    nd = arr.ndim
    return pl.BlockSpec(arr.shape, lambda *_: (0,) * nd, pipeline_mode=pl.Buffered(1))


def _softplus(z):
    return jnp.maximum(z, 0.0) + jnp.log1p(jnp.exp(-jnp.abs(z)))


def _segsum(q, ones_ref):
    hi = q.astype(BF16)
    lo = (q - hi.astype(F32)).astype(BF16)
    ones = ones_ref[...]
    return (jnp.dot(hi, ones, preferred_element_type=F32)
            + jnp.dot(lo, ones, preferred_element_type=F32))


def _rms_mod(x, nw, sc, sh):
    ms = jnp.mean(x * x, axis=-1, keepdims=True)
    h = x * lax.rsqrt(ms + NORM_EPS) * nw
    return h * (1.0 + sc) + sh


def _ada_kernel(c_ref, w_ref, b_ref, o_ref):
    c = c_ref[...]
    cs = (c * jax.nn.sigmoid(c)).astype(BF16)
    o_ref[...] = jnp.dot(cs, w_ref[...].astype(BF16), preferred_element_type=F32) + b_ref[...]


def _ada(c_all, w_ada, b_ada):
    n = c_all.shape[0]
    return pl.pallas_call(
        _ada_kernel,
        grid=(DEPTH, N_MOD),
        in_specs=[
            pl.BlockSpec((n, D_MODEL), lambda l, j: (0, 0)),
            pl.BlockSpec((None, D_MODEL, D_MODEL), lambda l, j: (l, 0, j)),
            pl.BlockSpec((None, 1, D_MODEL), lambda l, j: (l, 0, j)),
        ],
        out_specs=pl.BlockSpec((None, n, D_MODEL), lambda l, j: (l, 0, j)),
        out_shape=jax.ShapeDtypeStruct((DEPTH, n, N_MOD * D_MODEL), F32),
        compiler_params=pltpu.CompilerParams(dimension_semantics=("parallel", "parallel")),
        name="ada_mod",
    )(c_all, w_ada, b_ada.reshape(DEPTH, 1, N_MOD * D_MODEL))


def _row_spec(width):
    return pl.BlockSpec((TOKEN_TILE, width), lambda i: (i, 0))


def _mod_spec(col):
    return pl.BlockSpec((TOKEN_TILE, D_MODEL), lambda i: (0, col))


_TOKEN_PARAMS = pltpu.CompilerParams(dimension_semantics=("parallel",), vmem_limit_bytes=V7X_VMEM_LIMIT)


def _ffn_kernel(x_ref, sh_ref, sc_ref, g_ref, nw_ref, wg_ref, wu_ref, wd_ref, *rest, final):
    if final:
        fn_ref, o_ref = rest
    else:
        (o_ref,) = rest
    x = x_ref[...]
    hb = _rms_mod(x, nw_ref[...], sc_ref[...], sh_ref[...]).astype(BF16)
    acc = jnp.zeros(x.shape, F32)
    for c in range(D_FF // FF_CHUNK):
        lo, hi = c * FF_CHUNK, (c + 1) * FF_CHUNK
        g = jnp.dot(hb, wg_ref[:, lo:hi], preferred_element_type=F32)
        u = jnp.dot(hb, wu_ref[:, lo:hi], preferred_element_type=F32)
        a = (g * jax.nn.sigmoid(g) * u).astype(BF16)
        acc = acc + jnp.dot(a, wd_ref[lo:hi, :], preferred_element_type=F32)
    y = x + 0.5 * g_ref[...] * acc
    if final:
        ms = jnp.mean(y * y, axis=-1, keepdims=True)
        y = y * lax.rsqrt(ms + NORM_EPS) * fn_ref[...]
    o_ref[...] = y


def _ffn(x, mod, first_col, nw, wg, wu, wd, final_norm=None):
    final = final_norm is not None
    ins = [x, mod, mod, mod, nw, wg, wu, wd]
    specs = [_row_spec(D_MODEL), _mod_spec(first_col), _mod_spec(first_col + 1), _mod_spec(first_col + 2),
             _const_spec(nw), _const_spec(wg), _const_spec(wu), _const_spec(wd)]
    if final:
        ins.append(final_norm)
        specs.append(_const_spec(final_norm))
    return pl.pallas_call(
        functools.partial(_ffn_kernel, final=final),
        grid=(x.shape[0] // TOKEN_TILE,),
        in_specs=specs,
        out_specs=_row_spec(D_MODEL),
        out_shape=jax.ShapeDtypeStruct(x.shape, F32),
        compiler_params=_TOKEN_PARAMS,
        name="ffn_final" if final else "ffn",
    )(*ins)


def _inproj_kernel(x_ref, sh_ref, sc_ref, nw_ref, w_ref, o_ref):
    hb = _rms_mod(x_ref[...], nw_ref[...], sc_ref[...], sh_ref[...]).astype(BF16)
    o_ref[...] = jnp.dot(hb, w_ref[...], preferred_element_type=F32)


def _inproj(x, mod, nw, w_in):
    return pl.pallas_call(
        _inproj_kernel,
        grid=(x.shape[0] // TOKEN_TILE,),
        in_specs=[_row_spec(D_MODEL), _mod_spec(3), _mod_spec(4), _const_spec(nw), _const_spec(w_in)],
        out_specs=_row_spec(IN_W),
        out_shape=jax.ShapeDtypeStruct((x.shape[0], IN_W), F32),
        compiler_params=_TOKEN_PARAMS,
        name="mix_inproj",
    )(x, mod, mod, nw, w_in)


def _store_packed(ref, a, b):
    lane = lax.broadcasted_iota(jnp.int32, (a.shape[0], LANES), 1)
    first = lane < HEAD
    for hp in range(HEADS // 2):
        ap = a[:, hp * LANES:(hp + 1) * LANES]
        bp = b[:, hp * LANES:(hp + 1) * LANES]
        ref[:, 2 * hp, :] = jnp.where(first, ap, pltpu.roll(bp, HEAD, 1))
        ref[:, 2 * hp + 1, :] = jnp.where(first, pltpu.roll(ap, HEAD, 1), bp)


def _mix1_kernel(p_ref, conv0_ref, h0_ref, shift0_ref,
                 cw_ref, cb_ref, wx_ref, bx_ref, wa_ref, ba_ref, lam_ref,
                 mu_ref, w0_ref, w2_ref, a0_ref, a2_ref, g2_ref, kk_ref, ka_ref, rk_ref, ones_ref,
                 ylru_ref, rw_ref, ka_out_ref, bv_ref, g_ref, bonus_ref,
                 convn_ref, hn_ref, shiftn_ref,
                 xe_ref, pe_ref, hc_ref, a_s, b_s, h_s, *, nb_rows, steps):
    B, R = nb_rows, nb_rows * steps
    CB = (CONV_W - 1) * B

    @pl.when(pl.program_id(0) == 0)
    def _():
        xe_ref[0:CB, :] = conv0_ref[...]
        pe_ref[0:B, :] = shift0_ref[...]
        hc_ref[...] = h0_ref[...]

    xe_ref[CB:CB + R, :] = p_ref[:, 0:D_LRU]
    gb = p_ref[:, D_LRU:2 * D_LRU]
    xc = cb_ref[...] + xe_ref[CB:CB + R, :] * cw_ref[CONV_W - 1:CONV_W, :]
    for j in range(CONV_W - 1):
        xc = xc + xe_ref[j * B:j * B + R, :] * cw_ref[j:j + 1, :]
    xcb = xc.astype(BF16)
    gate_x = jax.nn.sigmoid(jnp.dot(xcb, wx_ref[...], preferred_element_type=F32) + bx_ref[...])
    gate_a = jax.nn.sigmoid(jnp.dot(xcb, wa_ref[...], preferred_element_type=F32) + ba_ref[...])
    log_a = -LRU_C * gate_a * _softplus(-lam_ref[...])
    a = jnp.exp(log_a)
    a_s[...] = a
    b_s[...] = jnp.sqrt(-jnp.tanh(log_a) * (a * a + 1.0)) * gate_x * xc

    def scan_step(t, carry):
        sl = pl.ds(pl.multiple_of(t * B, B), B)
        h = a_s[sl, :] * hc_ref[...] + b_s[sl, :]
        hc_ref[...] = h
        h_s[sl, :] = h
        return carry

    lax.fori_loop(0, steps, scan_step, 0)
    ylru_ref[...] = h_s[...] * jax.nn.gelu(gb)
    tail = xe_ref[R:R + CB, :]
    xe_ref[0:CB, :] = tail
    convn_ref[...] = tail
    hn_ref[...] = hc_ref[...]

    pr = p_ref[:, 2 * D_LRU:IN_W]
    pe_ref[B:B + R, :] = pr
    xs = pr + (pe_ref[0:R, :] - pr) * mu_ref[...]
    last = pe_ref[R:R + B, :]
    pe_ref[0:B, :] = last
    shiftn_ref[...] = last

    r = xs[:, 0:D_RWKV]
    k = xs[:, D_RWKV:2 * D_RWKV]
    v = xs[:, 2 * D_RWKV:3 * D_RWKV]
    lowrank = xs[:, 3 * D_RWKV:3 * D_RWKV + R_DECAY + R_ICLR]
    gd = xs[:, 3 * D_RWKV + R_DECAY + R_ICLR:RWKV_PROJ_W]
    wlin = w0_ref[...] + jnp.dot(jnp.tanh(lowrank).astype(BF16), w2_ref[...], preferred_element_type=F32)
    w_log = -_softplus(-wlin) - 0.5
    a_rate = jax.nn.sigmoid(a0_ref[...] + jnp.dot(lowrank.astype(BF16), a2_ref[...], preferred_element_type=F32))
    kkr = k * kk_ref[...]
    kkn = kkr / jnp.maximum(jnp.sqrt(_segsum(kkr * kkr, ones_ref)), L2_EPS)
    kf = k * (1.0 + (a_rate - 1.0) * ka_ref[...])
    _store_packed(rw_ref, r, jnp.exp(-jnp.exp(w_log)))
    _store_packed(ka_out_ref, kf, -kkn)
    _store_packed(bv_ref, kkn * a_rate, v)
    g_ref[...] = jnp.dot(jax.nn.sigmoid(gd).astype(BF16), g2_ref[...], preferred_element_type=F32)
    bonus_ref[...] = _segsum(r * kf * rk_ref[...], ones_ref) * v


def _mix1(proj, conv0, h0, shift0, params, nb_rows, n_steps):
    n = proj.shape[0]
    steps = MIX_ROWS // nb_rows
    R = steps * nb_rows
    CB = (CONV_W - 1) * nb_rows
    row = lambda w: pl.BlockSpec((R, w), lambda i: (i, 0))
    packed = pl.BlockSpec((R, HEADS, LANES), lambda i: (i, 0, 0))
    full = lambda a: pl.BlockSpec(a.shape, lambda i: (0,) * a.ndim)
    wide = jax.ShapeDtypeStruct((n, D_RWKV), F32)
    pk = jax.ShapeDtypeStruct((n, HEADS, LANES), F32)
    return pl.pallas_call(
        functools.partial(_mix1_kernel, nb_rows=nb_rows, steps=steps),
        grid=(n_steps // steps,),
        in_specs=[row(IN_W), full(conv0), full(h0), full(shift0)] + [_const_spec(p) for p in params],
        out_specs=[row(D_RWKV), packed, packed, packed, row(D_RWKV), row(D_RWKV),
                   full(conv0), full(h0), full(shift0)],
        out_shape=[wide, pk, pk, pk, wide, wide, jax.ShapeDtypeStruct(conv0.shape, F32),
                   jax.ShapeDtypeStruct(h0.shape, F32), jax.ShapeDtypeStruct(shift0.shape, F32)],
        scratch_shapes=[
            pltpu.VMEM((CB + R, D_LRU), F32),
            pltpu.VMEM((nb_rows + R, RWKV_PROJ_W), F32),
            pltpu.VMEM((nb_rows, D_LRU), F32),
            pltpu.VMEM((R, D_LRU), F32),
            pltpu.VMEM((R, D_LRU), F32),
            pltpu.VMEM((R, D_LRU), F32),
        ],
        compiler_params=pltpu.CompilerParams(dimension_semantics=("arbitrary",),
                                             vmem_limit_bytes=V7X_VMEM_LIMIT),
        name="mix_pre",
    )(proj, conv0, h0, shift0, *params)


def _wkv_kernel(rw_ref, ka_ref, bv_ref, s0_ref, y_ref, s_ref, *, nv, steps, split):
    chains = LANES // split

    @pl.when(pl.program_id(1) == 0)
    def _():
        s_ref[...] = s0_ref[...]

    lane_split = lax.broadcasted_iota(jnp.int32, (nv, LANES), 1) // chains

    def load_pair(ref, t):
        x = ref[t].reshape(chains, LANES)
        xt = jnp.concatenate([x] * split, axis=0).T
        return xt[0:HEAD], xt[HEAD:2 * HEAD]

    def step(t, carry):
        rt, wt = load_pair(rw_ref, t)
        kt, at = load_pair(ka_ref, t)
        bt, vall = load_pair(bv_ref, t)
        vt = vall[0:nv]
        for s in range(1, split):
            vt = jnp.where(lane_split == s, vall[s * nv:(s + 1) * nv], vt)
        rows = []
        for p in range(nv):
            st = s_ref[p]
            sa = jnp.sum(st * at, axis=0, keepdims=True)
            st = st * wt + sa * bt + vt[p:p + 1, :] * kt
            s_ref[p] = st
            rows.append(jnp.sum(st * rt, axis=0, keepdims=True))
        yt = jnp.concatenate(rows, axis=0).T
        yn = jnp.concatenate([yt[s * chains:(s + 1) * chains] for s in range(split)], axis=1)
        y_ref[t] = yn.reshape(chains // HEADS, HEADS, HEAD)
        return carry

    lax.fori_loop(0, steps, step, 0)


def _wkv(rw, ka, bv, s0_t, nb, n_steps, split, time_tile):
    nv = HEAD // split
    bg = LANES // (split * HEADS)
    op_spec = pl.BlockSpec((time_tile, bg, HEADS, LANES), lambda g, ti: (ti, g, 0, 0))
    y_spec = pl.BlockSpec((time_tile, bg, HEADS, HEAD), lambda g, ti: (ti, g, 0, 0))
    s_spec = pl.BlockSpec((nv, HEAD, LANES), lambda g, ti: (0, 0, g))
    return pl.pallas_call(
        functools.partial(_wkv_kernel, nv=nv, steps=time_tile, split=split),
        grid=(nb // bg, n_steps // time_tile),
        in_specs=[op_spec] * 3 + [s_spec],
        out_specs=[y_spec, s_spec],
        out_shape=[jax.ShapeDtypeStruct((n_steps, nb, HEADS, HEAD), F32), jax.ShapeDtypeStruct(s0_t.shape, F32)],
        compiler_params=pltpu.CompilerParams(dimension_semantics=("parallel", "arbitrary"),
                                             vmem_limit_bytes=V7X_VMEM_LIMIT),
        name="wkv_scan",
    )(rw, ka, bv, s0_t)


def _to_wkv_state(s, nb, split):
    bg = LANES // (split * HEADS)
    s = s.reshape(nb // bg, bg, HEADS, split, HEAD // split, HEAD).transpose(4, 5, 0, 3, 1, 2)
    return s.reshape(HEAD // split, HEAD, (nb // bg) * LANES)


def _from_wkv_state(s, nb, split):
    bg = LANES // (split * HEADS)
    s = s.reshape(HEAD // split, HEAD, nb // bg, split, bg, HEADS).transpose(2, 4, 5, 3, 0, 1)
    return s.reshape(nb, HEADS, HEAD, HEAD)


def _mix2_kernel(x_ref, g2_ref, ylru_ref, y_ref, bonus_ref, g_ref, lnw_ref, lnb_ref, ones_ref, wo_ref, o_ref):
    y = jnp.concatenate([y_ref[:, h, :] for h in range(HEADS)], axis=1)
    mu = _segsum(y, ones_ref) * (1.0 / HEAD)
    d = y - mu
    var = _segsum(d * d, ones_ref) * (1.0 / HEAD)
    yn = d * lax.rsqrt(var + GN_EPS) * lnw_ref[...] + lnb_ref[...]
    yrw = ((yn + bonus_ref[...]) * g_ref[...]).astype(BF16)
    m = (jnp.dot(ylru_ref[...].astype(BF16), wo_ref[0:D_LRU, :], preferred_element_type=F32)
         + jnp.dot(yrw, wo_ref[D_LRU:D_LRU + D_RWKV, :], preferred_element_type=F32))
    o_ref[...] = x_ref[...] + g2_ref[...] * m


def _mix2(x, mod, ylru, y, bonus, g, lnw, lnb, ones_bd, w_out):
    y_spec = pl.BlockSpec((TOKEN_TILE, HEADS, HEAD), lambda i: (i, 0, 0))
    return pl.pallas_call(
        _mix2_kernel,
        grid=(x.shape[0] // TOKEN_TILE,),
        in_specs=[_row_spec(D_MODEL), _mod_spec(5), _row_spec(D_LRU), y_spec, _row_spec(D_RWKV), _row_spec(D_RWKV),
                  _const_spec(lnw), _const_spec(lnb), _const_spec(ones_bd), _const_spec(w_out)],
        out_specs=_row_spec(D_MODEL),
        out_shape=jax.ShapeDtypeStruct(x.shape, F32),
        compiler_params=_TOKEN_PARAMS,
        name="mix_post",
    )(x, mod, ylru, y, bonus, g, lnw, lnb, ones_bd, w_out)


def _block_diag(w):
    n, c, d = w.shape
    eye = jnp.eye(n, dtype=w.dtype)
    return (eye[:, None, :, None] * w[:, :, None, :]).reshape(n * c, n * d)


def _row(v):
    return v.reshape(1, -1)


def _layer_params(P, l):
    z = jnp.zeros((R_DECAY, D_RWKV), F32)
    w2p = jnp.concatenate([P["rwkv_w2"][l], z], axis=0).astype(BF16)
    a2p = jnp.concatenate([z, P["rwkv_a2"][l]], axis=0).astype(BF16)
    ones_bd = _block_diag(jnp.ones((HEADS, HEAD, HEAD), F32)).astype(BF16)
    mix1 = [P["lru_conv_w"][l], _row(P["lru_conv_b"][l]),
            _block_diag(P["lru_wx"][l]).astype(BF16), _row(P["lru_bx"][l]),
            _block_diag(P["lru_wa"][l]).astype(BF16), _row(P["lru_ba"][l]), _row(P["lru_lambda"][l]),
            _row(P["rwkv_mu"][l]), _row(P["rwkv_w0"][l]), w2p, _row(P["rwkv_a0"][l]), a2p,
            P["rwkv_g2"][l].astype(BF16), _row(P["rwkv_k_k"][l]), _row(P["rwkv_k_a"][l]),
            _row(P["rwkv_r_k"][l]), ones_bd]
    return dict(
        ffn1=(_row(P["ffn1_norm"][l]), P["ffn1_w_gate"][l].astype(BF16), P["ffn1_w_up"][l].astype(BF16),
              P["ffn1_w_down"][l].astype(BF16)),
        ffn2=(_row(P["ffn2_norm"][l]), P["ffn2_w_gate"][l].astype(BF16), P["ffn2_w_up"][l].astype(BF16),
              P["ffn2_w_down"][l].astype(BF16)),
        mix_norm=_row(P["mix_norm"][l]), w_in=P["w_in"][l].astype(BF16), w_out=P["w_out"][l].astype(BF16),
        mix1=mix1, ln_w=_row(P["rwkv_ln_w"][l]), ln_b=_row(P["rwkv_ln_b"][l]), ones_bd=ones_bd)


def _trunk(x, mods, conv, hst, shift, wkv, layers, final_norm, nb, n_steps, split, time_tile):
    n_conv, n_h, n_shift, n_wkv = [], [], [], []
    for l, lp in enumerate(layers):
        mod = jnp.tile(mods[l], (TOKEN_TILE // nb, 1))
        x = _ffn(x, mod, 0, *lp["ffn1"])
        proj = _inproj(x, mod, lp["mix_norm"], lp["w_in"])
        conv0 = conv[l].transpose(1, 0, 2).reshape((CONV_W - 1) * nb, D_LRU)
        ylru, rw, ka, bv, g, bonus, convn, hn, shiftn = _mix1(proj, conv0, hst[l], shift[l], lp["mix1"], nb, n_steps)
        op4 = lambda a: a.reshape(n_steps, nb, HEADS, LANES)
        y, s_t = _wkv(op4(rw), op4(ka), op4(bv), _to_wkv_state(wkv[l], nb, split), nb, n_steps, split, time_tile)
        x = _mix2(x, mod, ylru, y.reshape(n_steps * nb, HEADS, HEAD), bonus, g,
                  lp["ln_w"], lp["ln_b"], lp["ones_bd"], lp["w_out"])
        x = _ffn(x, mod, 6, *lp["ffn2"], final_norm=final_norm if l == DEPTH - 1 else None)
        n_conv.append(convn.reshape(CONV_W - 1, nb, D_LRU).transpose(1, 0, 2))
        n_h.append(hn)
        n_shift.append(shiftn)
        n_wkv.append(_from_wkv_state(s_t, nb, split))
    return x, (jnp.stack(n_conv), jnp.stack(n_h), jnp.stack(n_shift), jnp.stack(n_wkv))


def kernel(x_prompt, x_sample, c_prompt, c_sample, state_lru_conv, state_lru_h, state_rwkv_shift, state_rwkv_wkv, w_ada, b_ada, ffn1_norm, ffn1_w_gate, ffn1_w_up, ffn1_w_down, mix_norm, w_in, w_out, lru_conv_w, lru_conv_b, lru_wx, lru_bx, lru_wa, lru_ba, lru_lambda, rwkv_mu, rwkv_w0, rwkv_w2, rwkv_a0, rwkv_a2, rwkv_g2, rwkv_k_k, rwkv_k_a, rwkv_r_k, rwkv_ln_w, rwkv_ln_b, ffn2_norm, ffn2_w_gate, ffn2_w_up, ffn2_w_down, final_norm):
    P = dict(ffn1_norm=ffn1_norm, ffn1_w_gate=ffn1_w_gate, ffn1_w_up=ffn1_w_up, ffn1_w_down=ffn1_w_down,
             mix_norm=mix_norm, w_in=w_in, w_out=w_out, lru_conv_w=lru_conv_w, lru_conv_b=lru_conv_b,
             lru_wx=lru_wx, lru_bx=lru_bx, lru_wa=lru_wa, lru_ba=lru_ba, lru_lambda=lru_lambda,
             rwkv_mu=rwkv_mu, rwkv_w0=rwkv_w0, rwkv_w2=rwkv_w2, rwkv_a0=rwkv_a0, rwkv_a2=rwkv_a2,
             rwkv_g2=rwkv_g2, rwkv_k_k=rwkv_k_k, rwkv_k_a=rwkv_k_a, rwkv_r_k=rwkv_r_k,
             rwkv_ln_w=rwkv_ln_w, rwkv_ln_b=rwkv_ln_b, ffn2_norm=ffn2_norm, ffn2_w_gate=ffn2_w_gate,
             ffn2_w_up=ffn2_w_up, ffn2_w_down=ffn2_w_down)
    layers = [_layer_params(P, l) for l in range(DEPTH)]
    fn = _row(final_norm)
    bp, tp, _ = x_prompt.shape
    bs, ts, _ = x_sample.shape

    mod = _ada(jnp.concatenate([c_prompt, c_sample], axis=0), w_ada, b_ada)

    def time_major(x):
        b, t, d = x.shape
        return x.transpose(1, 0, 2).reshape(t * b, d)

    zeros = lambda *s: jnp.zeros((DEPTH,) + s, F32)
    y_p, st_p = _trunk(time_major(x_prompt), mod[:, :bp], zeros(bp, CONV_W - 1, D_LRU), zeros(bp, D_LRU),
                       zeros(bp, RWKV_PROJ_W), zeros(bp, HEADS, HEAD, HEAD), layers, fn,
                       bp, tp, LANES // (bp * HEADS), WKV_TIME_TILE)
    y_s, st_s = _trunk(time_major(x_sample), mod[:, bp:], state_lru_conv, state_lru_h, state_rwkv_shift,
                       state_rwkv_wkv, layers, fn, bs, ts, 1, ts)

    y_prompt = y_p.reshape(tp, bp, D_MODEL).transpose(1, 0, 2)
    y_sample = y_s.reshape(ts, bs, D_MODEL).transpose(1, 0, 2)
    return (y_prompt, y_sample) + st_p + st_s
```

```python
import functools

import jax
import jax.numpy as jnp
from jax import lax
from jax.experimental import pallas as pl
from jax.experimental.pallas import tpu as pltpu

F32 = jnp.float32
BF16 = jnp.bfloat16

D_MODEL = 1024
DEPTH = 2
D_LRU = 512
LRU_BLOCKS = 8
CONV_W = 4
LRU_C = 8.0
D_RWKV = 512
HEAD = 64
HEADS = D_RWKV // HEAD
R_DECAY = 64
R_ICLR = 64
R_GATE = 128
RWKV_PROJ_W = 3 * D_RWKV + R_DECAY + R_ICLR + R_GATE
IN_W = 2 * D_LRU + RWKV_PROJ_W
D_FF = 2816
N_MOD = 9
NORM_EPS = 1e-6
GN_EPS = 64e-5
L2_EPS = 1e-12

LANES = 128
SUBLANES = 8
V7X_VMEM_LIMIT = 56 * 1024 * 1024
FF_CHUNK = 704
TOKEN_TILE = 512
MIX_ROWS = 256
WKV_TIME_TILE = 32

assert 2 * HEAD == LANES


def _const_spec(arr):
    nd = arr.ndim
    return pl.BlockSpec(arr.shape, lambda *_: (0,) * nd, pipeline_mode=pl.Buffered(1))


def _softplus(z):
    return jnp.maximum(z, 0.0) + jnp.log1p(jnp.exp(-jnp.abs(z)))


def _segsum(q, ones_ref):
    hi = q.astype(BF16)
    lo = (q - hi.astype(F32)).astype(BF16)
    ones = ones_ref[...]
    return (jnp.dot(hi, ones, preferred_element_type=F32)
            + jnp.dot(lo, ones, preferred_element_type=F32))


def _rms_mod(x, nw, sc, sh):
    ms = jnp.mean(x * x, axis=-1, keepdims=True)
    h = x * lax.rsqrt(ms + NORM_EPS) * nw
    return h * (1.0 + sc) + sh


def _ada_kernel(c_ref, w_ref, b_ref, o_ref):
    c = c_ref[...]
    cs = (c * jax.nn.sigmoid(c)).astype(BF16)
    o_ref[...] = jnp.dot(cs, w_ref[...].astype(BF16), preferred_element_type=F32) + b_ref[...]


def _ada(c_all, w_ada, b_ada):
    n = c_all.shape[0]
    return pl.pallas_call(
        _ada_kernel,
        grid=(DEPTH, N_MOD),
        in_specs=[
            pl.BlockSpec((n, D_MODEL), lambda l, j: (0, 0)),
            pl.BlockSpec((None, D_MODEL, D_MODEL), lambda l, j: (l, 0, j)),
            pl.BlockSpec((None, 1, D_MODEL), lambda l, j: (l, 0, j)),
        ],
        out_specs=pl.BlockSpec((None, n, D_MODEL), lambda l, j: (l, 0, j)),
        out_shape=jax.ShapeDtypeStruct((DEPTH, n, N_MOD * D_MODEL), F32),
        compiler_params=pltpu.CompilerParams(dimension_semantics=("parallel", "parallel")),
        name="ada_mod",
    )(c_all, w_ada, b_ada.reshape(DEPTH, 1, N_MOD * D_MODEL))


def _row_spec(width):
    return pl.BlockSpec((TOKEN_TILE, width), lambda i: (i, 0))


def _mod_spec(col):
    return pl.BlockSpec((TOKEN_TILE, D_MODEL), lambda i: (0, col))


_TOKEN_PARAMS = pltpu.CompilerParams(dimension_semantics=("parallel",), vmem_limit_bytes=V7X_VMEM_LIMIT)


def _ffn_kernel(x_ref, sh_ref, sc_ref, g_ref, nw_ref, wg_ref, wu_ref, wd_ref, *rest, final):
    if final:
        fn_ref, o_ref = rest
    else:
        (o_ref,) = rest
    x = x_ref[...]
    hb = _rms_mod(x, nw_ref[...], sc_ref[...], sh_ref[...]).astype(BF16)
    acc = jnp.zeros(x.shape, F32)
    for c in range(D_FF // FF_CHUNK):
        lo, hi = c * FF_CHUNK, (c + 1) * FF_CHUNK
        g = jnp.dot(hb, wg_ref[:, lo:hi], preferred_element_type=F32)
        u = jnp.dot(hb, wu_ref[:, lo:hi], preferred_element_type=F32)
        a = (g * jax.nn.sigmoid(g) * u).astype(BF16)
        acc = acc + jnp.dot(a, wd_ref[lo:hi, :], preferred_element_type=F32)
    y = x + 0.5 * g_ref[...] * acc
    if final:
        ms = jnp.mean(y * y, axis=-1, keepdims=True)
        y = y * lax.rsqrt(ms + NORM_EPS) * fn_ref[...]
    o_ref[...] = y


def _ffn(x, mod, first_col, nw, wg, wu, wd, final_norm=None):
    final = final_norm is not None
    ins = [x, mod, mod, mod, nw, wg, wu, wd]
    specs = [_row_spec(D_MODEL), _mod_spec(first_col), _mod_spec(first_col + 1), _mod_spec(first_col + 2),
             _const_spec(nw), _const_spec(wg), _const_spec(wu), _const_spec(wd)]
    if final:
        ins.append(final_norm)
        specs.append(_const_spec(final_norm))
    return pl.pallas_call(
        functools.partial(_ffn_kernel, final=final),
        grid=(x.shape[0] // TOKEN_TILE,),
        in_specs=specs,
        out_specs=_row_spec(D_MODEL),
        out_shape=jax.ShapeDtypeStruct(x.shape, F32),
        compiler_params=_TOKEN_PARAMS,
        name="ffn_final" if final else "ffn",
    )(*ins)


def _inproj_kernel(x_ref, sh_ref, sc_ref, nw_ref, w_ref, o_ref):
    hb = _rms_mod(x_ref[...], nw_ref[...], sc_ref[...], sh_ref[...]).astype(BF16)
    o_ref[...] = jnp.dot(hb, w_ref[...], preferred_element_type=F32)


def _inproj(x, mod, nw, w_in):
    return pl.pallas_call(
        _inproj_kernel,
        grid=(x.shape[0] // TOKEN_TILE,),
        in_specs=[_row_spec(D_MODEL), _mod_spec(3), _mod_spec(4), _const_spec(nw), _const_spec(w_in)],
        out_specs=_row_spec(IN_W),
        out_shape=jax.ShapeDtypeStruct((x.shape[0], IN_W), F32),
        compiler_params=_TOKEN_PARAMS,
        name="mix_inproj",
    )(x, mod, mod, nw, w_in)


def _store_packed(ref, a, b):
    lane = lax.broadcasted_iota(jnp.int32, (a.shape[0], LANES), 1)
    first = lane < HEAD
    for hp in range(HEADS // 2):
        ap = a[:, hp * LANES:(hp + 1) * LANES]
        bp = b[:, hp * LANES:(hp + 1) * LANES]
        ref[:, 2 * hp, :] = jnp.where(first, ap, pltpu.roll(bp, HEAD, 1))
        ref[:, 2 * hp + 1, :] = jnp.where(first, pltpu.roll(ap, HEAD, 1), bp)


def _mix1_kernel(p_ref, conv0_ref, h0_ref, shift0_ref,
                 cw_ref, cb_ref, wx_ref, bx_ref, wa_ref, ba_ref, lam_ref,
                 mu_ref, w0_ref, w2_ref, a0_ref, a2_ref, g2_ref, kk_ref, ka_ref, rk_ref, ones_ref,
                 ylru_ref, rw_ref, ka_out_ref, bv_ref, g_ref, bonus_ref,
                 convn_ref, hn_ref, shiftn_ref,
                 xe_ref, pe_ref, hc_ref, a_s, b_s, h_s, *, nb_rows, steps):
    B, R = nb_rows, nb_rows * steps
    CB = (CONV_W - 1) * B

    @pl.when(pl.program_id(0) == 0)
    def _():
        xe_ref[0:CB, :] = conv0_ref[...]
        pe_ref[0:B, :] = shift0_ref[...]
        hc_ref[...] = h0_ref[...]

    xe_ref[CB:CB + R, :] = p_ref[:, 0:D_LRU]
    gb = p_ref[:, D_LRU:2 * D_LRU]
    xc = cb_ref[...] + xe_ref[CB:CB + R, :] * cw_ref[CONV_W - 1:CONV_W, :]
    for j in range(CONV_W - 1):
        xc = xc + xe_ref[j * B:j * B + R, :] * cw_ref[j:j + 1, :]
    xcb = xc.astype(BF16)
    gate_x = jax.nn.sigmoid(jnp.dot(xcb, wx_ref[...], preferred_element_type=F32) + bx_ref[...])
    gate_a = jax.nn.sigmoid(jnp.dot(xcb, wa_ref[...], preferred_element_type=F32) + ba_ref[...])
    log_a = -LRU_C * gate_a * _softplus(-lam_ref[...])
    a = jnp.exp(log_a)
    a_s[...] = a
    b_s[...] = jnp.sqrt(-jnp.tanh(log_a) * (a * a + 1.0)) * gate_x * xc

    def scan_step(t, carry):
        sl = pl.ds(pl.multiple_of(t * B, B), B)
        h = a_s[sl, :] * hc_ref[...] + b_s[sl, :]
        hc_ref[...] = h
        h_s[sl, :] = h
        return carry

    lax.fori_loop(0, steps, scan_step, 0)
    ylru_ref[...] = h_s[...] * jax.nn.gelu(gb)
    tail = xe_ref[R:R + CB, :]
    xe_ref[0:CB, :] = tail
    convn_ref[...] = tail
    hn_ref[...] = hc_ref[...]

    pr = p_ref[:, 2 * D_LRU:IN_W]
    pe_ref[B:B + R, :] = pr
    xs = pr + (pe_ref[0:R, :] - pr) * mu_ref[...]
    last = pe_ref[R:R + B, :]
    pe_ref[0:B, :] = last
    shiftn_ref[...] = last

    r = xs[:, 0:D_RWKV]
    k = xs[:, D_RWKV:2 * D_RWKV]
    v = xs[:, 2 * D_RWKV:3 * D_RWKV]
    lowrank = xs[:, 3 * D_RWKV:3 * D_RWKV + R_DECAY + R_ICLR]
    gd = xs[:, 3 * D_RWKV + R_DECAY + R_ICLR:RWKV_PROJ_W]
    wlin = w0_ref[...] + jnp.dot(jnp.tanh(lowrank).astype(BF16), w2_ref[...], preferred_element_type=F32)
    w_log = -_softplus(-wlin) - 0.5
    a_rate = jax.nn.sigmoid(a0_ref[...] + jnp.dot(lowrank.astype(BF16), a2_ref[...], preferred_element_type=F32))
    kkr = k * kk_ref[...]
    kkn = kkr / jnp.maximum(jnp.sqrt(_segsum(kkr * kkr, ones_ref)), L2_EPS)
    kf = k * (1.0 + (a_rate - 1.0) * ka_ref[...])
    _store_packed(rw_ref, r, jnp.exp(-jnp.exp(w_log)))
    _store_packed(ka_out_ref, kf, -kkn)
    _store_packed(bv_ref, kkn * a_rate, v)
    g_ref[...] = jnp.dot(jax.nn.sigmoid(gd).astype(BF16), g2_ref[...], preferred_element_type=F32)
    bonus_ref[...] = _segsum(r * kf * rk_ref[...], ones_ref) * v


def _mix1(proj, conv0, h0, shift0, params, nb_rows, n_steps):
    n = proj.shape[0]
    steps = MIX_ROWS // nb_rows
    R = steps * nb_rows
    CB = (CONV_W - 1) * nb_rows
    row = lambda w: pl.BlockSpec((R, w), lambda i: (i, 0))
    packed = pl.BlockSpec((R, HEADS, LANES), lambda i: (i, 0, 0))
    full = lambda a: pl.BlockSpec(a.shape, lambda i: (0,) * a.ndim)
    wide = jax.ShapeDtypeStruct((n, D_RWKV), F32)
    pk = jax.ShapeDtypeStruct((n, HEADS, LANES), F32)
    return pl.pallas_call(
        functools.partial(_mix1_kernel, nb_rows=nb_rows, steps=steps),
        grid=(n_steps // steps,),
        in_specs=[row(IN_W), full(conv0), full(h0), full(shift0)] + [_const_spec(p) for p in params],
        out_specs=[row(D_RWKV), packed, packed, packed, row(D_RWKV), row(D_RWKV),
                   full(conv0), full(h0), full(shift0)],
        out_shape=[wide, pk, pk, pk, wide, wide, jax.ShapeDtypeStruct(conv0.shape, F32),
                   jax.ShapeDtypeStruct(h0.shape, F32), jax.ShapeDtypeStruct(shift0.shape, F32)],
        scratch_shapes=[
            pltpu.VMEM((CB + R, D_LRU), F32),
            pltpu.VMEM((nb_rows + R, RWKV_PROJ_W), F32),
            pltpu.VMEM((nb_rows, D_LRU), F32),
            pltpu.VMEM((R, D_LRU), F32),
            pltpu.VMEM((R, D_LRU), F32),
            pltpu.VMEM((R, D_LRU), F32),
        ],
        compiler_params=pltpu.CompilerParams(dimension_semantics=("arbitrary",),
                                             vmem_limit_bytes=V7X_VMEM_LIMIT),
        name="mix_pre",
    )(proj, conv0, h0, shift0, *params)


def _wkv_kernel(rw_ref, ka_ref, bv_ref, s0_ref, y_ref, s_ref, ops_s, y_s, *, nv, steps, split):
    chains = LANES // split

    @pl.when(pl.program_id(1) == 0)
    def _():
        s_ref[...] = s0_ref[...]

    lane_split = lax.broadcasted_iota(jnp.int32, (nv, LANES), 1) // chains

    def stage_operands(t, slot):
        for j, ref in enumerate((rw_ref, ka_ref, bv_ref)):
            x = ref[t].reshape(chains, LANES)
            ops_s[slot, j] = jnp.concatenate([x] * split, axis=0).T

    def emit_y(t, slot):
        yt = y_s[slot].T
        yn = jnp.concatenate([yt[s * chains:(s + 1) * chains] for s in range(split)], axis=1)
        y_ref[t] = yn.reshape(chains // HEADS, HEADS, HEAD)

    def compute(slot):
        rt, wt = ops_s[slot, 0, 0:HEAD], ops_s[slot, 0, HEAD:2 * HEAD]
        kt, at = ops_s[slot, 1, 0:HEAD], ops_s[slot, 1, HEAD:2 * HEAD]
        bt, vall = ops_s[slot, 2, 0:HEAD], ops_s[slot, 2, HEAD:2 * HEAD]
        vt = vall[0:nv]
        for s in range(1, split):
            vt = jnp.where(lane_split == s, vall[s * nv:(s + 1) * nv], vt)
        rows = []
        for p in range(nv):
            st = s_ref[p]
            sa = jnp.sum(st * at, axis=0, keepdims=True)
            st = st * wt + sa * bt + vt[p:p + 1, :] * kt
            s_ref[p] = st
            rows.append(jnp.sum(st * rt, axis=0, keepdims=True))
        y_s[slot] = jnp.concatenate(rows, axis=0)

    stage_operands(0, 0)
    y_s[1] = jnp.zeros((nv, LANES), F32)

    def step_pair(i, carry):
        t0 = 2 * i
        stage_operands(t0 + 1, 1)
        compute(0)
        emit_y(jnp.maximum(t0 - 1, 0), 1)
        stage_operands(jnp.minimum(t0 + 2, steps - 1), 0)
        compute(1)
        emit_y(t0, 0)
        return carry

    lax.fori_loop(0, steps // 2, step_pair, 0)
    emit_y(steps - 1, 1)


def _wkv(rw, ka, bv, s0_t, nb, n_steps, split, time_tile):
    nv = HEAD // split
    bg = LANES // (split * HEADS)
    op_spec = pl.BlockSpec((time_tile, bg, HEADS, LANES), lambda g, ti: (ti, g, 0, 0))
    y_spec = pl.BlockSpec((time_tile, bg, HEADS, HEAD), lambda g, ti: (ti, g, 0, 0))
    s_spec = pl.BlockSpec((nv, HEAD, LANES), lambda g, ti: (0, 0, g))
    return pl.pallas_call(
        functools.partial(_wkv_kernel, nv=nv, steps=time_tile, split=split),
        grid=(nb // bg, n_steps // time_tile),
        in_specs=[op_spec] * 3 + [s_spec],
        out_specs=[y_spec, s_spec],
        out_shape=[jax.ShapeDtypeStruct((n_steps, nb, HEADS, HEAD), F32), jax.ShapeDtypeStruct(s0_t.shape, F32)],
        scratch_shapes=[pltpu.VMEM((2, 3, LANES, LANES), F32), pltpu.VMEM((2, nv, LANES), F32)],
        compiler_params=pltpu.CompilerParams(dimension_semantics=("parallel", "arbitrary"),
                                             vmem_limit_bytes=V7X_VMEM_LIMIT),
        name="wkv_scan",
    )(rw, ka, bv, s0_t)


def _to_wkv_state(s, nb, split):
    bg = LANES // (split * HEADS)
    s = s.reshape(nb // bg, bg, HEADS, split, HEAD // split, HEAD).transpose(4, 5, 0, 3, 1, 2)
    return s.reshape(HEAD // split, HEAD, (nb // bg) * LANES)


def _from_wkv_state(s, nb, split):
    bg = LANES // (split * HEADS)
    s = s.reshape(HEAD // split, HEAD, nb // bg, split, bg, HEADS).transpose(2, 4, 5, 3, 0, 1)
    return s.reshape(nb, HEADS, HEAD, HEAD)


def _mix2_kernel(x_ref, g2_ref, ylru_ref, y_ref, bonus_ref, g_ref, lnw_ref, lnb_ref, ones_ref, wo_ref, o_ref):
    y = jnp.concatenate([y_ref[:, h, :] for h in range(HEADS)], axis=1)
    mu = _segsum(y, ones_ref) * (1.0 / HEAD)
    d = y - mu
    var = _segsum(d * d, ones_ref) * (1.0 / HEAD)
    yn = d * lax.rsqrt(var + GN_EPS) * lnw_ref[...] + lnb_ref[...]
    yrw = ((yn + bonus_ref[...]) * g_ref[...]).astype(BF16)
    m = (jnp.dot(ylru_ref[...].astype(BF16), wo_ref[0:D_LRU, :], preferred_element_type=F32)
         + jnp.dot(yrw, wo_ref[D_LRU:D_LRU + D_RWKV, :], preferred_element_type=F32))
    o_ref[...] = x_ref[...] + g2_ref[...] * m


def _mix2(x, mod, ylru, y, bonus, g, lnw, lnb, ones_bd, w_out):
    y_spec = pl.BlockSpec((TOKEN_TILE, HEADS, HEAD), lambda i: (i, 0, 0))
    return pl.pallas_call(
        _mix2_kernel,
        grid=(x.shape[0] // TOKEN_TILE,),
        in_specs=[_row_spec(D_MODEL), _mod_spec(5), _row_spec(D_LRU), y_spec, _row_spec(D_RWKV), _row_spec(D_RWKV),
                  _const_spec(lnw), _const_spec(lnb), _const_spec(ones_bd), _const_spec(w_out)],
        out_specs=_row_spec(D_MODEL),
        out_shape=jax.ShapeDtypeStruct(x.shape, F32),
        compiler_params=_TOKEN_PARAMS,
        name="mix_post",
    )(x, mod, ylru, y, bonus, g, lnw, lnb, ones_bd, w_out)


def _block_diag(w):
    n, c, d = w.shape
    eye = jnp.eye(n, dtype=w.dtype)
    return (eye[:, None, :, None] * w[:, :, None, :]).reshape(n * c, n * d)


def _row(v):
    return v.reshape(1, -1)


def _layer_params(P, l):
    z = jnp.zeros((R_DECAY, D_RWKV), F32)
    w2p = jnp.concatenate([P["rwkv_w2"][l], z], axis=0).astype(BF16)
    a2p = jnp.concatenate([z, P["rwkv_a2"][l]], axis=0).astype(BF16)
    ones_bd = _block_diag(jnp.ones((HEADS, HEAD, HEAD), F32)).astype(BF16)
    mix1 = [P["lru_conv_w"][l], _row(P["lru_conv_b"][l]),
            _block_diag(P["lru_wx"][l]).astype(BF16), _row(P["lru_bx"][l]),
            _block_diag(P["lru_wa"][l]).astype(BF16), _row(P["lru_ba"][l]), _row(P["lru_lambda"][l]),
            _row(P["rwkv_mu"][l]), _row(P["rwkv_w0"][l]), w2p, _row(P["rwkv_a0"][l]), a2p,
            P["rwkv_g2"][l].astype(BF16), _row(P["rwkv_k_k"][l]), _row(P["rwkv_k_a"][l]),
            _row(P["rwkv_r_k"][l]), ones_bd]
    return dict(
        ffn1=(_row(P["ffn1_norm"][l]), P["ffn1_w_gate"][l].astype(BF16), P["ffn1_w_up"][l].astype(BF16),
              P["ffn1_w_down"][l].astype(BF16)),
        ffn2=(_row(P["ffn2_norm"][l]), P["ffn2_w_gate"][l].astype(BF16), P["ffn2_w_up"][l].astype(BF16),
              P["ffn2_w_down"][l].astype(BF16)),
        mix_norm=_row(P["mix_norm"][l]), w_in=P["w_in"][l].astype(BF16), w_out=P["w_out"][l].astype(BF16),
        mix1=mix1, ln_w=_row(P["rwkv_ln_w"][l]), ln_b=_row(P["rwkv_ln_b"][l]), ones_bd=ones_bd)


def _trunk(x, mods, conv, hst, shift, wkv, layers, final_norm, nb, n_steps, split, time_tile):
    n_conv, n_h, n_shift, n_wkv = [], [], [], []
    for l, lp in enumerate(layers):
        mod = jnp.tile(mods[l], (TOKEN_TILE // nb, 1))
        x = _ffn(x, mod, 0, *lp["ffn1"])
        proj = _inproj(x, mod, lp["mix_norm"], lp["w_in"])
        conv0 = conv[l].transpose(1, 0, 2).reshape((CONV_W - 1) * nb, D_LRU)
        ylru, rw, ka, bv, g, bonus, convn, hn, shiftn = _mix1(proj, conv0, hst[l], shift[l], lp["mix1"], nb, n_steps)
        op4 = lambda a: a.reshape(n_steps, nb, HEADS, LANES)
        y, s_t = _wkv(op4(rw), op4(ka), op4(bv), _to_wkv_state(wkv[l], nb, split), nb, n_steps, split, time_tile)
        x = _mix2(x, mod, ylru, y.reshape(n_steps * nb, HEADS, HEAD), bonus, g,
                  lp["ln_w"], lp["ln_b"], lp["ones_bd"], lp["w_out"])
        x = _ffn(x, mod, 6, *lp["ffn2"], final_norm=final_norm if l == DEPTH - 1 else None)
        n_conv.append(convn.reshape(CONV_W - 1, nb, D_LRU).transpose(1, 0, 2))
        n_h.append(hn)
        n_shift.append(shiftn)
        n_wkv.append(_from_wkv_state(s_t, nb, split))
    return x, (jnp.stack(n_conv), jnp.stack(n_h), jnp.stack(n_shift), jnp.stack(n_wkv))


def kernel(x_prompt, x_sample, c_prompt, c_sample, state_lru_conv, state_lru_h, state_rwkv_shift, state_rwkv_wkv, w_ada, b_ada, ffn1_norm, ffn1_w_gate, ffn1_w_up, ffn1_w_down, mix_norm, w_in, w_out, lru_conv_w, lru_conv_b, lru_wx, lru_bx, lru_wa, lru_ba, lru_lambda, rwkv_mu, rwkv_w0, rwkv_w2, rwkv_a0, rwkv_a2, rwkv_g2, rwkv_k_k, rwkv_k_a, rwkv_r_k, rwkv_ln_w, rwkv_ln_b, ffn2_norm, ffn2_w_gate, ffn2_w_up, ffn2_w_down, final_norm):
    P = dict(ffn1_norm=ffn1_norm, ffn1_w_gate=ffn1_w_gate, ffn1_w_up=ffn1_w_up, ffn1_w_down=ffn1_w_down,
             mix_norm=mix_norm, w_in=w_in, w_out=w_out, lru_conv_w=lru_conv_w, lru_conv_b=lru_conv_b,
             lru_wx=lru_wx, lru_bx=lru_bx, lru_wa=lru_wa, lru_ba=lru_ba, lru_lambda=lru_lambda,
             rwkv_mu=rwkv_mu, rwkv_w0=rwkv_w0, rwkv_w2=rwkv_w2, rwkv_a0=rwkv_a0, rwkv_a2=rwkv_a2,
             rwkv_g2=rwkv_g2, rwkv_k_k=rwkv_k_k, rwkv_k_a=rwkv_k_a, rwkv_r_k=rwkv_r_k,
             rwkv_ln_w=rwkv_ln_w, rwkv_ln_b=rwkv_ln_b, ffn2_norm=ffn2_norm, ffn2_w_gate=ffn2_w_gate,
             ffn2_w_up=ffn2_w_up, ffn2_w_down=ffn2_w_down)
    layers = [_layer_params(P, l) for l in range(DEPTH)]
    fn = _row(final_norm)
    bp, tp, _ = x_prompt.shape
    bs, ts, _ = x_sample.shape

    mod = _ada(jnp.concatenate([c_prompt, c_sample], axis=0), w_ada, b_ada)

    def time_major(x):
        b, t, d = x.shape
        return x.transpose(1, 0, 2).reshape(t * b, d)

    zeros = lambda *s: jnp.zeros((DEPTH,) + s, F32)
    y_p, st_p = _trunk(time_major(x_prompt), mod[:, :bp], zeros(bp, CONV_W - 1, D_LRU), zeros(bp, D_LRU),
                       zeros(bp, RWKV_PROJ_W), zeros(bp, HEADS, HEAD, HEAD), layers, fn,
                       bp, tp, LANES // (bp * HEADS), WKV_TIME_TILE)
    y_s, st_s = _trunk(time_major(x_sample), mod[:, bp:], state_lru_conv, state_lru_h, state_rwkv_shift,
                       state_rwkv_wkv, layers, fn, bs, ts, 1, ts)

    y_prompt = y_p.reshape(tp, bp, D_MODEL).transpose(1, 0, 2)
    y_sample = y_s.reshape(ts, bs, D_MODEL).transpose(1, 0, 2)
    return (y_prompt, y_sample) + st_p + st_s
```

```python
import functools

import jax
import jax.numpy as jnp
from jax import lax
from jax.experimental import pallas as pl
from jax.experimental.pallas import tpu as pltpu

F32 = jnp.float32
BF16 = jnp.bfloat16

D_MODEL = 1024
DEPTH = 2
D_LRU = 512
LRU_BLOCKS = 8
CONV_W = 4
LRU_C = 8.0
D_RWKV = 512
HEAD = 64
HEADS = D_RWKV // HEAD
R_DECAY = 64
R_ICLR = 64
R_GATE = 128
RWKV_PROJ_W = 3 * D_RWKV + R_DECAY + R_ICLR + R_GATE
IN_W = 2 * D_LRU + RWKV_PROJ_W
D_FF = 2816
N_MOD = 9
NORM_EPS = 1e-6
GN_EPS = 64e-5
L2_EPS = 1e-12

LANES = 128
SUBLANES = 8
V7X_VMEM_LIMIT = 56 * 1024 * 1024
FF_CHUNK = 704
TOKEN_TILE = 512
MIX_ROWS = 256
WKV_TIME_TILE = 32

assert 2 * HEAD == LANES


def _const_spec(arr):
    nd = arr.ndim
    return pl.BlockSpec(arr.shape, lambda *_: (0,) * nd, pipeline_mode=pl.Buffered(1))


def _softplus(z):
    return jnp.maximum(z, 0.0) + jnp.log1p(jnp.exp(-jnp.abs(z)))


def _segsum(q, ones_ref):
    hi = q.astype(BF16)
    lo = (q - hi.astype(F32)).astype(BF16)
    ones = ones_ref[...]
    return (jnp.dot(hi, ones, preferred_element_type=F32)
            + jnp.dot(lo, ones, preferred_element_type=F32))


def _rms_mod(x, nw, sc, sh):
    ms = jnp.mean(x * x, axis=-1, keepdims=True)
    h = x * lax.rsqrt(ms + NORM_EPS) * nw
    return h * (1.0 + sc) + sh


def _ada_kernel(c_ref, w_ref, b_ref, o_ref):
    c = c_ref[...]
    cs = (c * jax.nn.sigmoid(c)).astype(BF16)
    o_ref[...] = jnp.dot(cs, w_ref[...].astype(BF16), preferred_element_type=F32) + b_ref[...]


def _ada(c_all, w_ada, b_ada):
    n = c_all.shape[0]
    return pl.pallas_call(
        _ada_kernel,
        grid=(DEPTH, N_MOD),
        in_specs=[
            pl.BlockSpec((n, D_MODEL), lambda l, j: (0, 0)),
            pl.BlockSpec((None, D_MODEL, D_MODEL), lambda l, j: (l, 0, j)),
            pl.BlockSpec((None, 1, D_MODEL), lambda l, j: (l, 0, j)),
        ],
        out_specs=pl.BlockSpec((None, n, D_MODEL), lambda l, j: (l, 0, j)),
        out_shape=jax.ShapeDtypeStruct((DEPTH, n, N_MOD * D_MODEL), F32),
        compiler_params=pltpu.CompilerParams(dimension_semantics=("parallel", "parallel")),
        name="ada_mod",
    )(c_all, w_ada, b_ada.reshape(DEPTH, 1, N_MOD * D_MODEL))


def _row_spec(width):
    return pl.BlockSpec((TOKEN_TILE, width), lambda i: (i, 0))


def _mod_spec(col):
    return pl.BlockSpec((TOKEN_TILE, D_MODEL), lambda i: (0, col))


_TOKEN_PARAMS = pltpu.CompilerParams(dimension_semantics=("parallel",), vmem_limit_bytes=V7X_VMEM_LIMIT)


def _ffn_kernel(x_ref, sh_ref, sc_ref, g_ref, nw_ref, wg_ref, wu_ref, wd_ref, *rest, final):
    if final:
        fn_ref, o_ref = rest
    else:
        (o_ref,) = rest
    x = x_ref[...]
    hb = _rms_mod(x, nw_ref[...], sc_ref[...], sh_ref[...]).astype(BF16)
    acc = jnp.zeros(x.shape, F32)
    for c in range(D_FF // FF_CHUNK):
        lo, hi = c * FF_CHUNK, (c + 1) * FF_CHUNK
        g = jnp.dot(hb, wg_ref[:, lo:hi], preferred_element_type=F32)
        u = jnp.dot(hb, wu_ref[:, lo:hi], preferred_element_type=F32)
        a = (g * jax.nn.sigmoid(g) * u).astype(BF16)
        acc = acc + jnp.dot(a, wd_ref[lo:hi, :], preferred_element_type=F32)
    y = x + 0.5 * g_ref[...] * acc
    if final:
        ms = jnp.mean(y * y, axis=-1, keepdims=True)
        y = y * lax.rsqrt(ms + NORM_EPS) * fn_ref[...]
    o_ref[...] = y


def _ffn(x, mod, first_col, nw, wg, wu, wd, final_norm=None):
    final = final_norm is not None
    ins = [x, mod, mod, mod, nw, wg, wu, wd]
    specs = [_row_spec(D_MODEL), _mod_spec(first_col), _mod_spec(first_col + 1), _mod_spec(first_col + 2),
             _const_spec(nw), _const_spec(wg), _const_spec(wu), _const_spec(wd)]
    if final:
        ins.append(final_norm)
        specs.append(_const_spec(final_norm))
    return pl.pallas_call(
        functools.partial(_ffn_kernel, final=final),
        grid=(x.shape[0] // TOKEN_TILE,),
        in_specs=specs,
        out_specs=_row_spec(D_MODEL),
        out_shape=jax.ShapeDtypeStruct(x.shape, F32),
        compiler_params=_TOKEN_PARAMS,
        name="ffn_final" if final else "ffn",
    )(*ins)


def _inproj_kernel(x_ref, sh_ref, sc_ref, nw_ref, w_ref, o_ref):
    hb = _rms_mod(x_ref[...], nw_ref[...], sc_ref[...], sh_ref[...]).astype(BF16)
    o_ref[...] = jnp.dot(hb, w_ref[...], preferred_element_type=F32)


def _inproj(x, mod, nw, w_in):
    return pl.pallas_call(
        _inproj_kernel,
        grid=(x.shape[0] // TOKEN_TILE,),
        in_specs=[_row_spec(D_MODEL), _mod_spec(3), _mod_spec(4), _const_spec(nw), _const_spec(w_in)],
        out_specs=_row_spec(IN_W),
        out_shape=jax.ShapeDtypeStruct((x.shape[0], IN_W), F32),
        compiler_params=_TOKEN_PARAMS,
        name="mix_inproj",
    )(x, mod, mod, nw, w_in)


def _store_packed(ref, a, b):
    steps, _, nb, _ = ref.shape
    lane = lax.broadcasted_iota(jnp.int32, (a.shape[0], LANES), 1)
    first = lane < HEAD
    for hp in range(HEADS // 2):
        ap = a[:, hp * LANES:(hp + 1) * LANES]
        bp = b[:, hp * LANES:(hp + 1) * LANES]
        ref[:, 2 * hp] = jnp.where(first, ap, pltpu.roll(bp, HEAD, 1)).reshape(steps, nb, LANES)
        ref[:, 2 * hp + 1] = jnp.where(first, pltpu.roll(ap, HEAD, 1), bp).reshape(steps, nb, LANES)


def _mix1_kernel(p_ref, conv0_ref, h0_ref, shift0_ref,
                 cw_ref, cb_ref, wx_ref, bx_ref, wa_ref, ba_ref, lam_ref,
                 mu_ref, w0_ref, w2_ref, a0_ref, a2_ref, g2_ref, kk_ref, ka_ref, rk_ref, ones_ref,
                 ylru_ref, rw_ref, ka_out_ref, bv_ref, g_ref, bonus_ref,
                 convn_ref, hn_ref, shiftn_ref,
                 xe_ref, pe_ref, hc_ref, a_s, b_s, h_s, *, nb_rows, steps):
    B, R = nb_rows, nb_rows * steps
    CB = (CONV_W - 1) * B

    @pl.when(pl.program_id(0) == 0)
    def _():
        xe_ref[0:CB, :] = conv0_ref[...]
        pe_ref[0:B, :] = shift0_ref[...]
        hc_ref[...] = h0_ref[...]

    xe_ref[CB:CB + R, :] = p_ref[:, 0:D_LRU]
    gb = p_ref[:, D_LRU:2 * D_LRU]
    xc = cb_ref[...] + xe_ref[CB:CB + R, :] * cw_ref[CONV_W - 1:CONV_W, :]
    for j in range(CONV_W - 1):
        xc = xc + xe_ref[j * B:j * B + R, :] * cw_ref[j:j + 1, :]
    xcb = xc.astype(BF16)
    gate_x = jax.nn.sigmoid(jnp.dot(xcb, wx_ref[...], preferred_element_type=F32) + bx_ref[...])
    gate_a = jax.nn.sigmoid(jnp.dot(xcb, wa_ref[...], preferred_element_type=F32) + ba_ref[...])
    log_a = -LRU_C * gate_a * _softplus(-lam_ref[...])
    a = jnp.exp(log_a)
    a_s[...] = a
    b_s[...] = jnp.sqrt(1.0 - a * a) * gate_x * xc

    def scan_step(t, carry):
        sl = pl.ds(pl.multiple_of(t * B, B), B)
        h = a_s[sl, :] * hc_ref[...] + b_s[sl, :]
        hc_ref[...] = h
        h_s[sl, :] = h
        return carry

    lax.fori_loop(0, steps, scan_step, 0)
    ylru_ref[...] = h_s[...] * jax.nn.gelu(gb)
    tail = xe_ref[R:R + CB, :]
    xe_ref[0:CB, :] = tail
    convn_ref[...] = tail
    hn_ref[...] = hc_ref[...]

    pr = p_ref[:, 2 * D_LRU:IN_W]
    pe_ref[B:B + R, :] = pr
    xs = pr + (pe_ref[0:R, :] - pr) * mu_ref[...]
    last = pe_ref[R:R + B, :]
    pe_ref[0:B, :] = last
    shiftn_ref[...] = last

    r = xs[:, 0:D_RWKV]
    k = xs[:, D_RWKV:2 * D_RWKV]
    v = xs[:, 2 * D_RWKV:3 * D_RWKV]
    lowrank = xs[:, 3 * D_RWKV:3 * D_RWKV + R_DECAY + R_ICLR]
    gd = xs[:, 3 * D_RWKV + R_DECAY + R_ICLR:RWKV_PROJ_W]
    wlin = w0_ref[...] + jnp.dot(jnp.tanh(lowrank).astype(BF16), w2_ref[...], preferred_element_type=F32)
    w_log = -_softplus(-wlin) - 0.5
    a_rate = jax.nn.sigmoid(a0_ref[...] + jnp.dot(lowrank.astype(BF16), a2_ref[...], preferred_element_type=F32))
    kkr = k * kk_ref[...]
    kkn = kkr / jnp.maximum(jnp.sqrt(_segsum(kkr * kkr, ones_ref)), L2_EPS)
    kf = k * (1.0 + (a_rate - 1.0) * ka_ref[...])
    _store_packed(rw_ref, r, jnp.exp(-jnp.exp(w_log)))
    _store_packed(ka_out_ref, kf, -kkn)
    _store_packed(bv_ref, kkn * a_rate, v)
    g_ref[...] = jnp.dot(jax.nn.sigmoid(gd).astype(BF16), g2_ref[...], preferred_element_type=F32)
    bonus_ref[...] = _segsum(r * kf * rk_ref[...], ones_ref) * v


def _mix1(proj, conv0, h0, shift0, params, nb_rows, n_steps):
    n = proj.shape[0]
    steps = MIX_ROWS // nb_rows
    R = steps * nb_rows
    CB = (CONV_W - 1) * nb_rows
    row = lambda w: pl.BlockSpec((R, w), lambda i: (i, 0))
    packed = pl.BlockSpec((steps, HEADS, nb_rows, LANES), lambda i: (i, 0, 0, 0))
    full = lambda a: pl.BlockSpec(a.shape, lambda i: (0,) * a.ndim)
    wide = jax.ShapeDtypeStruct((n, D_RWKV), F32)
    pk = jax.ShapeDtypeStruct((n_steps, HEADS, nb_rows, LANES), F32)
    return pl.pallas_call(
        functools.partial(_mix1_kernel, nb_rows=nb_rows, steps=steps),
        grid=(n_steps // steps,),
        in_specs=[row(IN_W), full(conv0), full(h0), full(shift0)] + [_const_spec(p) for p in params],
        out_specs=[row(D_RWKV), packed, packed, packed, row(D_RWKV), row(D_RWKV),
                   full(conv0), full(h0), full(shift0)],
        out_shape=[wide, pk, pk, pk, wide, wide, jax.ShapeDtypeStruct(conv0.shape, F32),
                   jax.ShapeDtypeStruct(h0.shape, F32), jax.ShapeDtypeStruct(shift0.shape, F32)],
        scratch_shapes=[
            pltpu.VMEM((CB + R, D_LRU), F32),
            pltpu.VMEM((nb_rows + R, RWKV_PROJ_W), F32),
            pltpu.VMEM((nb_rows, D_LRU), F32),
            pltpu.VMEM((R, D_LRU), F32),
            pltpu.VMEM((R, D_LRU), F32),
            pltpu.VMEM((R, D_LRU), F32),
        ],
        compiler_params=pltpu.CompilerParams(dimension_semantics=("arbitrary",),
                                             vmem_limit_bytes=V7X_VMEM_LIMIT),
        name="mix_pre",
    )(proj, conv0, h0, shift0, *params)


def _wkv_kernel(rw_ref, ka_ref, bv_ref, s0_ref, y_ref, s_ref, ops_s, y_s, *, nv, steps, split):
    chains = LANES // split

    @pl.when(pl.program_id(1) == 0)
    def _():
        s_ref[...] = s0_ref[...]

    lane_split = lax.broadcasted_iota(jnp.int32, (nv, LANES), 1) // chains

    def stage_operands(t, slot):
        for j, ref in enumerate((rw_ref, ka_ref, bv_ref)):
            x = ref[t].reshape(chains, LANES)
            ops_s[slot, j] = jnp.concatenate([x] * split, axis=0).T

    def emit_y(t, slot):
        yt = y_s[slot].T
        yn = jnp.concatenate([yt[s * chains:(s + 1) * chains] for s in range(split)], axis=1)
        y_ref[t] = yn.reshape(HEADS, chains // HEADS, HEAD)

    def compute(slot):
        rt, wt = ops_s[slot, 0, 0:HEAD], ops_s[slot, 0, HEAD:2 * HEAD]
        kt, at = ops_s[slot, 1, 0:HEAD], ops_s[slot, 1, HEAD:2 * HEAD]
        bt, vall = ops_s[slot, 2, 0:HEAD], ops_s[slot, 2, HEAD:2 * HEAD]
        vt = vall[0:nv]
        for s in range(1, split):
            vt = jnp.where(lane_split == s, vall[s * nv:(s + 1) * nv], vt)
        rows = []
        for p in range(nv):
            st = s_ref[p]
            sa = jnp.sum(st * at, axis=0, keepdims=True)
            st = st * wt + sa * bt + vt[p:p + 1, :] * kt
            s_ref[p] = st
            rows.append(jnp.sum(st * rt, axis=0, keepdims=True))
        y_s[slot] = jnp.concatenate(rows, axis=0)

    stage_operands(0, 0)
    y_s[1] = jnp.zeros((nv, LANES), F32)

    def step_pair(i, carry):
        t0 = 2 * i
        stage_operands(t0 + 1, 1)
        compute(0)
        emit_y(jnp.maximum(t0 - 1, 0), 1)
        stage_operands(jnp.minimum(t0 + 2, steps - 1), 0)
        compute(1)
        emit_y(t0, 0)
        return carry

    lax.fori_loop(0, steps // 2, step_pair, 0)
    emit_y(steps - 1, 1)


def _wkv(rw, ka, bv, s0_t, nb, n_steps, split, time_tile):
    nv = HEAD // split
    bg = LANES // (split * HEADS)
    op_spec = pl.BlockSpec((time_tile, HEADS, bg, LANES), lambda g, ti: (ti, 0, g, 0))
    y_spec = pl.BlockSpec((time_tile, HEADS, bg, HEAD), lambda g, ti: (ti, 0, g, 0))
    s_spec = pl.BlockSpec((nv, HEAD, LANES), lambda g, ti: (0, 0, g))
    return pl.pallas_call(
        functools.partial(_wkv_kernel, nv=nv, steps=time_tile, split=split),
        grid=(nb // bg, n_steps // time_tile),
        in_specs=[op_spec] * 3 + [s_spec],
        out_specs=[y_spec, s_spec],
        out_shape=[jax.ShapeDtypeStruct((n_steps, HEADS, nb, HEAD), F32), jax.ShapeDtypeStruct(s0_t.shape, F32)],
        scratch_shapes=[pltpu.VMEM((2, 3, LANES, LANES), F32), pltpu.VMEM((2, nv, LANES), F32)],
        compiler_params=pltpu.CompilerParams(dimension_semantics=("parallel", "arbitrary"),
                                             vmem_limit_bytes=V7X_VMEM_LIMIT),
        name="wkv_scan",
    )(rw, ka, bv, s0_t)


def _to_wkv_state(s, nb, split):
    bg = LANES // (split * HEADS)
    s = s.reshape(nb // bg, bg, HEADS, split, HEAD // split, HEAD).transpose(4, 5, 0, 3, 2, 1)
    return s.reshape(HEAD // split, HEAD, (nb // bg) * LANES)


def _from_wkv_state(s, nb, split):
    bg = LANES // (split * HEADS)
    s = s.reshape(HEAD // split, HEAD, nb // bg, split, HEADS, bg).transpose(2, 5, 4, 3, 0, 1)
    return s.reshape(nb, HEADS, HEAD, HEAD)


def _mix2_kernel(x_ref, g2_ref, ylru_ref, y_ref, bonus_ref, g_ref, lnw_ref, lnb_ref, ones_ref, wo_ref, o_ref):
    y = jnp.concatenate([y_ref[:, h].reshape(TOKEN_TILE, HEAD) for h in range(HEADS)], axis=1)
    mu = _segsum(y, ones_ref) * (1.0 / HEAD)
    d = y - mu
    var = _segsum(d * d, ones_ref) * (1.0 / HEAD)
    yn = d * lax.rsqrt(var + GN_EPS) * lnw_ref[...] + lnb_ref[...]
    yrw = ((yn + bonus_ref[...]) * g_ref[...]).astype(BF16)
    m = (jnp.dot(ylru_ref[...].astype(BF16), wo_ref[0:D_LRU, :], preferred_element_type=F32)
         + jnp.dot(yrw, wo_ref[D_LRU:D_LRU + D_RWKV, :], preferred_element_type=F32))
    o_ref[...] = x_ref[...] + g2_ref[...] * m


def _mix2(x, mod, ylru, y, bonus, g, lnw, lnb, ones_bd, w_out):
    nb = y.shape[2]
    y_spec = pl.BlockSpec((TOKEN_TILE // nb, HEADS, nb, HEAD), lambda i: (i, 0, 0, 0))
    return pl.pallas_call(
        _mix2_kernel,
        grid=(x.shape[0] // TOKEN_TILE,),
        in_specs=[_row_spec(D_MODEL), _mod_spec(5), _row_spec(D_LRU), y_spec, _row_spec(D_RWKV), _row_spec(D_RWKV),
                  _const_spec(lnw), _const_spec(lnb), _const_spec(ones_bd), _const_spec(w_out)],
        out_specs=_row_spec(D_MODEL),
        out_shape=jax.ShapeDtypeStruct(x.shape, F32),
        compiler_params=_TOKEN_PARAMS,
        name="mix_post",
    )(x, mod, ylru, y, bonus, g, lnw, lnb, ones_bd, w_out)


def _block_diag(w):
    n, c, d = w.shape
    eye = jnp.eye(n, dtype=w.dtype)
    return (eye[:, None, :, None] * w[:, :, None, :]).reshape(n * c, n * d)


def _row(v):
    return v.reshape(1, -1)


def _layer_params(P, l):
    z = jnp.zeros((R_DECAY, D_RWKV), F32)
    w2p = jnp.concatenate([P["rwkv_w2"][l], z], axis=0).astype(BF16)
    a2p = jnp.concatenate([z, P["rwkv_a2"][l]], axis=0).astype(BF16)
    ones_bd = _block_diag(jnp.ones((HEADS, HEAD, HEAD), F32)).astype(BF16)
    mix1 = [P["lru_conv_w"][l], _row(P["lru_conv_b"][l]),
            _block_diag(P["lru_wx"][l]).astype(BF16), _row(P["lru_bx"][l]),
            _block_diag(P["lru_wa"][l]).astype(BF16), _row(P["lru_ba"][l]), _row(P["lru_lambda"][l]),
            _row(P["rwkv_mu"][l]), _row(P["rwkv_w0"][l]), w2p, _row(P["rwkv_a0"][l]), a2p,
            P["rwkv_g2"][l].astype(BF16), _row(P["rwkv_k_k"][l]), _row(P["rwkv_k_a"][l]),
            _row(P["rwkv_r_k"][l]), ones_bd]
    return dict(
        ffn1=(_row(P["ffn1_norm"][l]), P["ffn1_w_gate"][l].astype(BF16), P["ffn1_w_up"][l].astype(BF16),
              P["ffn1_w_down"][l].astype(BF16)),
        ffn2=(_row(P["ffn2_norm"][l]), P["ffn2_w_gate"][l].astype(BF16), P["ffn2_w_up"][l].astype(BF16),
              P["ffn2_w_down"][l].astype(BF16)),
        mix_norm=_row(P["mix_norm"][l]), w_in=P["w_in"][l].astype(BF16), w_out=P["w_out"][l].astype(BF16),
        mix1=mix1, ln_w=_row(P["rwkv_ln_w"][l]), ln_b=_row(P["rwkv_ln_b"][l]), ones_bd=ones_bd)


def _trunk(x, mods, conv, hst, shift, wkv, layers, final_norm, nb, n_steps, split, time_tile):
    n_conv, n_h, n_shift, n_wkv = [], [], [], []
    for l, lp in enumerate(layers):
        mod = jnp.tile(mods[l], (TOKEN_TILE // nb, 1))
        x = _ffn(x, mod, 0, *lp["ffn1"])
        proj = _inproj(x, mod, lp["mix_norm"], lp["w_in"])
        conv0 = conv[l].transpose(1, 0, 2).reshape((CONV_W - 1) * nb, D_LRU)
        ylru, rw, ka, bv, g, bonus, convn, hn, shiftn = _mix1(proj, conv0, hst[l], shift[l], lp["mix1"], nb, n_steps)
        y, s_t = _wkv(rw, ka, bv, _to_wkv_state(wkv[l], nb, split), nb, n_steps, split, time_tile)
        x = _mix2(x, mod, ylru, y, bonus, g,
                  lp["ln_w"], lp["ln_b"], lp["ones_bd"], lp["w_out"])
        x = _ffn(x, mod, 6, *lp["ffn2"], final_norm=final_norm if l == DEPTH - 1 else None)
        n_conv.append(convn.reshape(CONV_W - 1, nb, D_LRU).transpose(1, 0, 2))
        n_h.append(hn)
        n_shift.append(shiftn)
        n_wkv.append(_from_wkv_state(s_t, nb, split))
    return x, (jnp.stack(n_conv), jnp.stack(n_h), jnp.stack(n_shift), jnp.stack(n_wkv))


def kernel(x_prompt, x_sample, c_prompt, c_sample, state_lru_conv, state_lru_h, state_rwkv_shift, state_rwkv_wkv, w_ada, b_ada, ffn1_norm, ffn1_w_gate, ffn1_w_up, ffn1_w_down, mix_norm, w_in, w_out, lru_conv_w, lru_conv_b, lru_wx, lru_bx, lru_wa, lru_ba, lru_lambda, rwkv_mu, rwkv_w0, rwkv_w2, rwkv_a0, rwkv_a2, rwkv_g2, rwkv_k_k, rwkv_k_a, rwkv_r_k, rwkv_ln_w, rwkv_ln_b, ffn2_norm, ffn2_w_gate, ffn2_w_up, ffn2_w_down, final_norm):
    P = dict(ffn1_norm=ffn1_norm, ffn1_w_gate=ffn1_w_gate, ffn1_w_up=ffn1_w_up, ffn1_w_down=ffn1_w_down,
             mix_norm=mix_norm, w_in=w_in, w_out=w_out, lru_conv_w=lru_conv_w, lru_conv_b=lru_conv_b,
             lru_wx=lru_wx, lru_bx=lru_bx, lru_wa=lru_wa, lru_ba=lru_ba, lru_lambda=lru_lambda,
             rwkv_mu=rwkv_mu, rwkv_w0=rwkv_w0, rwkv_w2=rwkv_w2, rwkv_a0=rwkv_a0, rwkv_a2=rwkv_a2,
             rwkv_g2=rwkv_g2, rwkv_k_k=rwkv_k_k, rwkv_k_a=rwkv_k_a, rwkv_r_k=rwkv_r_k,
             rwkv_ln_w=rwkv_ln_w, rwkv_ln_b=rwkv_ln_b, ffn2_norm=ffn2_norm, ffn2_w_gate=ffn2_w_gate,
             ffn2_w_up=ffn2_w_up, ffn2_w_down=ffn2_w_down)
    layers = [_layer_params(P, l) for l in range(DEPTH)]
    fn = _row(final_norm)
    bp, tp, _ = x_prompt.shape
    bs, ts, _ = x_sample.shape

    mod = _ada(jnp.concatenate([c_prompt, c_sample], axis=0), w_ada, b_ada)

    def time_major(x):
        b, t, d = x.shape
        return x.transpose(1, 0, 2).reshape(t * b, d)

    zeros = lambda *s: jnp.zeros((DEPTH,) + s, F32)
    y_p, st_p = _trunk(time_major(x_prompt), mod[:, :bp], zeros(bp, CONV_W - 1, D_LRU), zeros(bp, D_LRU),
                       zeros(bp, RWKV_PROJ_W), zeros(bp, HEADS, HEAD, HEAD), layers, fn,
                       bp, tp, LANES // (bp * HEADS), WKV_TIME_TILE)
    y_s, st_s = _trunk(time_major(x_sample), mod[:, bp:], state_lru_conv, state_lru_h, state_rwkv_shift,
                       state_rwkv_wkv, layers, fn, bs, ts, 1, ts)

    y_prompt = y_p.reshape(tp, bp, D_MODEL).transpose(1, 0, 2)
    y_sample = y_s.reshape(ts, bs, D_MODEL).transpose(1, 0, 2)
    return (y_prompt, y_sample) + st_p + st_s
```

```python
import functools

import jax
import jax.numpy as jnp
from jax import lax
from jax.experimental import pallas as pl
from jax.experimental.pallas import tpu as pltpu

F32 = jnp.float32
BF16 = jnp.bfloat16

D_MODEL = 1024
DEPTH = 2
D_LRU = 512
LRU_BLOCKS = 8
CONV_W = 4
LRU_C = 8.0
D_RWKV = 512
HEAD = 64
HEADS = D_RWKV // HEAD
R_DECAY = 64
R_ICLR = 64
R_GATE = 128
RWKV_PROJ_W = 3 * D_RWKV + R_DECAY + R_ICLR + R_GATE
IN_W = 2 * D_LRU + RWKV_PROJ_W
D_FF = 2816
N_MOD = 9
NORM_EPS = 1e-6
GN_EPS = 64e-5
L2_EPS = 1e-12

LANES = 128
SUBLANES = 8
V7X_VMEM_LIMIT = 56 * 1024 * 1024
MXU_TILE = 256
FF_CHUNKS = (768, 768, 768, 512)
assert sum(FF_CHUNKS) == D_FF and all(c % MXU_TILE == 0 for c in FF_CHUNKS)
TOKEN_TILE = 512
MIX_ROWS = 256
WKV_TIME_TILE = 64

assert 2 * HEAD == LANES


def _const_spec(arr):
    nd = arr.ndim
    return pl.BlockSpec(arr.shape, lambda *_: (0,) * nd, pipeline_mode=pl.Buffered(1))


def _softplus(z):
    return jnp.maximum(z, 0.0) + jnp.log1p(jnp.exp(-jnp.abs(z)))


def _segsum(q, ones_ref):
    hi = q.astype(BF16)
    lo = (q - hi.astype(F32)).astype(BF16)
    ones = ones_ref[...]
    return (jnp.dot(hi, ones, preferred_element_type=F32)
            + jnp.dot(lo, ones, preferred_element_type=F32))


def _rms_mod(x, nw, sc, sh):
    ms = jnp.mean(x * x, axis=-1, keepdims=True)
    h = x * lax.rsqrt(ms + NORM_EPS) * nw
    return h * (1.0 + sc) + sh


def _ada_kernel(c_ref, w_ref, b_ref, o_ref):
    c = c_ref[...]
    cs = (c * jax.nn.sigmoid(c)).astype(BF16)
    o_ref[...] = jnp.dot(cs, w_ref[...].astype(BF16), preferred_element_type=F32) + b_ref[...]


def _ada(c_all, w_ada, b_ada):
    n = c_all.shape[0]
    return pl.pallas_call(
        _ada_kernel,
        grid=(DEPTH, N_MOD),
        in_specs=[
            pl.BlockSpec((n, D_MODEL), lambda l, j: (0, 0)),
            pl.BlockSpec((None, D_MODEL, D_MODEL), lambda l, j: (l, 0, j)),
            pl.BlockSpec((None, 1, D_MODEL), lambda l, j: (l, 0, j)),
        ],
        out_specs=pl.BlockSpec((None, n, D_MODEL), lambda l, j: (l, 0, j)),
        out_shape=jax.ShapeDtypeStruct((DEPTH, n, N_MOD * D_MODEL), F32),
        compiler_params=pltpu.CompilerParams(dimension_semantics=("parallel", "parallel")),
        name="ada_mod",
    )(c_all, w_ada, b_ada.reshape(DEPTH, 1, N_MOD * D_MODEL))


def _row_spec(width):
    return pl.BlockSpec((TOKEN_TILE, width), lambda i: (i, 0))


def _mod_spec(col):
    return pl.BlockSpec((TOKEN_TILE, D_MODEL), lambda i: (0, col))


_TOKEN_PARAMS = pltpu.CompilerParams(dimension_semantics=("parallel",), vmem_limit_bytes=V7X_VMEM_LIMIT)


def _ffn_kernel(x_ref, sh_ref, sc_ref, g_ref, nw_ref, wg_ref, wu_ref, wd_ref, *rest, final):
    if final:
        fn_ref, o_ref = rest
    else:
        (o_ref,) = rest
    x = x_ref[...]
    hb = _rms_mod(x, nw_ref[...], sc_ref[...], sh_ref[...]).astype(BF16)
    acc = jnp.zeros(x.shape, F32)
    hi = 0
    for width in FF_CHUNKS:
        lo, hi = hi, hi + width
        g = jnp.dot(hb, wg_ref[:, lo:hi], preferred_element_type=F32)
        u = jnp.dot(hb, wu_ref[:, lo:hi], preferred_element_type=F32)
        a = (g * jax.nn.sigmoid(g) * u).astype(BF16)
        acc = acc + jnp.dot(a, wd_ref[lo:hi, :], preferred_element_type=F32)
    y = x + 0.5 * g_ref[...] * acc
    if final:
        ms = jnp.mean(y * y, axis=-1, keepdims=True)
        y = y * lax.rsqrt(ms + NORM_EPS) * fn_ref[...]
    o_ref[...] = y


def _ffn(x, mod, first_col, nw, wg, wu, wd, final_norm=None):
    final = final_norm is not None
    ins = [x, mod, mod, mod, nw, wg, wu, wd]
    specs = [_row_spec(D_MODEL), _mod_spec(first_col), _mod_spec(first_col + 1), _mod_spec(first_col + 2),
             _const_spec(nw), _const_spec(wg), _const_spec(wu), _const_spec(wd)]
    if final:
        ins.append(final_norm)
        specs.append(_const_spec(final_norm))
    return pl.pallas_call(
        functools.partial(_ffn_kernel, final=final),
        grid=(x.shape[0] // TOKEN_TILE,),
        in_specs=specs,
        out_specs=_row_spec(D_MODEL),
        out_shape=jax.ShapeDtypeStruct(x.shape, F32),
        compiler_params=_TOKEN_PARAMS,
        name="ffn_final" if final else "ffn",
    )(*ins)


def _inproj_kernel(x_ref, sh_ref, sc_ref, nw_ref, w_ref, o_ref):
    hb = _rms_mod(x_ref[...], nw_ref[...], sc_ref[...], sh_ref[...]).astype(BF16)
    o_ref[...] = jnp.dot(hb, w_ref[...], preferred_element_type=F32)


def _inproj(x, mod, nw, w_in):
    return pl.pallas_call(
        _inproj_kernel,
        grid=(x.shape[0] // TOKEN_TILE,),
        in_specs=[_row_spec(D_MODEL), _mod_spec(3), _mod_spec(4), _const_spec(nw), _const_spec(w_in)],
        out_specs=_row_spec(IN_W),
        out_shape=jax.ShapeDtypeStruct((x.shape[0], IN_W), F32),
        compiler_params=_TOKEN_PARAMS,
        name="mix_inproj",
    )(x, mod, mod, nw, w_in)


def _store_packed(ref, a, b):
    steps, _, nb, _ = ref.shape
    lane = lax.broadcasted_iota(jnp.int32, (a.shape[0], LANES), 1)
    first = lane < HEAD
    for hp in range(HEADS // 2):
        ap = a[:, hp * LANES:(hp + 1) * LANES]
        bp = b[:, hp * LANES:(hp + 1) * LANES]
        ref[:, 2 * hp] = jnp.where(first, ap, pltpu.roll(bp, HEAD, 1)).reshape(steps, nb, LANES)
        ref[:, 2 * hp + 1] = jnp.where(first, pltpu.roll(ap, HEAD, 1), bp).reshape(steps, nb, LANES)


def _mix1_kernel(p_ref, conv0_ref, h0_ref, shift0_ref,
                 cw_ref, cb_ref, wx_ref, bx_ref, wa_ref, ba_ref, lam_ref,
                 mu_ref, w0_ref, w2_ref, a0_ref, a2_ref, g2_ref, kk_ref, ka_ref, rk_ref, ones_ref,
                 ylru_ref, rw_ref, ka_out_ref, bv_ref, g_ref, bonus_ref,
                 convn_ref, hn_ref, shiftn_ref,
                 xe_ref, pe_ref, hc_ref, a_s, b_s, h_s, *, nb_rows, steps):
    B, R = nb_rows, nb_rows * steps
    CB = (CONV_W - 1) * B

    @pl.when(pl.program_id(0) == 0)
    def _():
        xe_ref[0:CB, :] = conv0_ref[...]
        pe_ref[0:B, :] = shift0_ref[...]
        hc_ref[...] = h0_ref[...]

    xe_ref[CB:CB + R, :] = p_ref[:, 0:D_LRU]
    gb = p_ref[:, D_LRU:2 * D_LRU]
    xc = cb_ref[...] + xe_ref[CB:CB + R, :] * cw_ref[CONV_W - 1:CONV_W, :]
    for j in range(CONV_W - 1):
        xc = xc + xe_ref[j * B:j * B + R, :] * cw_ref[j:j + 1, :]
    xcb = xc.astype(BF16)
    gate_x = jax.nn.sigmoid(jnp.dot(xcb, wx_ref[...], preferred_element_type=F32) + bx_ref[...])
    gate_a = jax.nn.sigmoid(jnp.dot(xcb, wa_ref[...], preferred_element_type=F32) + ba_ref[...])
    log_a = -LRU_C * gate_a * _softplus(-lam_ref[...])
    a = jnp.exp(log_a)
    a_s[...] = a
    b_s[...] = jnp.sqrt(1.0 - a * a) * gate_x * xc

    def scan_step(t, carry):
        sl = pl.ds(pl.multiple_of(t * B, B), B)
        h = a_s[sl, :] * hc_ref[...] + b_s[sl, :]
        hc_ref[...] = h
        h_s[sl, :] = h
        return carry

    lax.fori_loop(0, steps, scan_step, 0)
    ylru_ref[...] = h_s[...] * jax.nn.gelu(gb)
    tail = xe_ref[R:R + CB, :]
    xe_ref[0:CB, :] = tail
    convn_ref[...] = tail
    hn_ref[...] = hc_ref[...]

    pr = p_ref[:, 2 * D_LRU:IN_W]
    pe_ref[B:B + R, :] = pr
    xs = pr + (pe_ref[0:R, :] - pr) * mu_ref[...]
    last = pe_ref[R:R + B, :]
    pe_ref[0:B, :] = last
    shiftn_ref[...] = last

    r = xs[:, 0:D_RWKV]
    k = xs[:, D_RWKV:2 * D_RWKV]
    v = xs[:, 2 * D_RWKV:3 * D_RWKV]
    lowrank = xs[:, 3 * D_RWKV:3 * D_RWKV + R_DECAY + R_ICLR]
    gd = xs[:, 3 * D_RWKV + R_DECAY + R_ICLR:RWKV_PROJ_W]
    wlin = w0_ref[...] + jnp.dot(jnp.tanh(lowrank).astype(BF16), w2_ref[...], preferred_element_type=F32)
    w_log = -_softplus(-wlin) - 0.5
    a_rate = jax.nn.sigmoid(a0_ref[...] + jnp.dot(lowrank.astype(BF16), a2_ref[...], preferred_element_type=F32))
    kkr = k * kk_ref[...]
    kkn = kkr / jnp.maximum(jnp.sqrt(_segsum(kkr * kkr, ones_ref)), L2_EPS)
    kf = k * (1.0 + (a_rate - 1.0) * ka_ref[...])
    _store_packed(rw_ref, r, jnp.exp(-jnp.exp(w_log)))
    _store_packed(ka_out_ref, kf, -kkn)
    _store_packed(bv_ref, kkn * a_rate, v)
    g_ref[...] = jnp.dot(jax.nn.sigmoid(gd).astype(BF16), g2_ref[...], preferred_element_type=F32)
    bonus_ref[...] = _segsum(r * kf * rk_ref[...], ones_ref) * v


def _mix1(proj, conv0, h0, shift0, params, nb_rows, n_steps):
    n = proj.shape[0]
    steps = MIX_ROWS // nb_rows
    R = steps * nb_rows
    CB = (CONV_W - 1) * nb_rows
    row = lambda w: pl.BlockSpec((R, w), lambda i: (i, 0))
    packed = pl.BlockSpec((steps, HEADS, nb_rows, LANES), lambda i: (i, 0, 0, 0))
    full = lambda a: pl.BlockSpec(a.shape, lambda i: (0,) * a.ndim)
    wide = jax.ShapeDtypeStruct((n, D_RWKV), F32)
    pk = jax.ShapeDtypeStruct((n_steps, HEADS, nb_rows, LANES), F32)
    return pl.pallas_call(
        functools.partial(_mix1_kernel, nb_rows=nb_rows, steps=steps),
        grid=(n_steps // steps,),
        in_specs=[row(IN_W), full(conv0), full(h0), full(shift0)] + [_const_spec(p) for p in params],
        out_specs=[row(D_RWKV), packed, packed, packed, row(D_RWKV), row(D_RWKV),
                   full(conv0), full(h0), full(shift0)],
        out_shape=[wide, pk, pk, pk, wide, wide, jax.ShapeDtypeStruct(conv0.shape, F32),
                   jax.ShapeDtypeStruct(h0.shape, F32), jax.ShapeDtypeStruct(shift0.shape, F32)],
        scratch_shapes=[
            pltpu.VMEM((CB + R, D_LRU), F32),
            pltpu.VMEM((nb_rows + R, RWKV_PROJ_W), F32),
            pltpu.VMEM((nb_rows, D_LRU), F32),
            pltpu.VMEM((R, D_LRU), F32),
            pltpu.VMEM((R, D_LRU), F32),
            pltpu.VMEM((R, D_LRU), F32),
        ],
        compiler_params=pltpu.CompilerParams(dimension_semantics=("arbitrary",),
                                             vmem_limit_bytes=V7X_VMEM_LIMIT),
        name="mix_pre",
    )(proj, conv0, h0, shift0, *params)


def _wkv_kernel(rw_ref, ka_ref, bv_ref, s0_ref, y_ref, s_ref, ops_s, y_s, *, nv, steps, split):
    chains = LANES // split

    @pl.when(pl.program_id(1) == 0)
    def _():
        s_ref[...] = s0_ref[...]

    lane_split = lax.broadcasted_iota(jnp.int32, (nv, LANES), 1) // chains

    def stage_operands(t, slot):
        for j, ref in enumerate((rw_ref, ka_ref, bv_ref)):
            x = ref[t].reshape(chains, LANES)
            ops_s[slot, j] = jnp.concatenate([x] * split, axis=0).T

    def emit_y(t, slot):
        yt = y_s[slot].T
        yn = jnp.concatenate([yt[s * chains:(s + 1) * chains] for s in range(split)], axis=1)
        y_ref[t] = yn.reshape(HEADS, chains // HEADS, HEAD)

    def compute(slot):
        rt, wt = ops_s[slot, 0, 0:HEAD], ops_s[slot, 0, HEAD:2 * HEAD]
        kt, at = ops_s[slot, 1, 0:HEAD], ops_s[slot, 1, HEAD:2 * HEAD]
        bt, vall = ops_s[slot, 2, 0:HEAD], ops_s[slot, 2, HEAD:2 * HEAD]
        vt = vall[0:nv]
        for s in range(1, split):
            vt = jnp.where(lane_split == s, vall[s * nv:(s + 1) * nv], vt)
        rows = []
        for p in range(nv):
            st = s_ref[p]
            sa = jnp.sum(st * at, axis=0, keepdims=True)
            st = st * wt + sa * bt + vt[p:p + 1, :] * kt
            s_ref[p] = st
            rows.append(jnp.sum(st * rt, axis=0, keepdims=True))
        y_s[slot] = jnp.concatenate(rows, axis=0)

    stage_operands(0, 0)
    y_s[1] = jnp.zeros((nv, LANES), F32)

    def step_pair(i, carry):
        t0 = 2 * i
        stage_operands(t0 + 1, 1)
        compute(0)
        emit_y(jnp.maximum(t0 - 1, 0), 1)
        stage_operands(jnp.minimum(t0 + 2, steps - 1), 0)
        compute(1)
        emit_y(t0, 0)
        return carry

    lax.fori_loop(0, steps // 2, step_pair, 0)
    emit_y(steps - 1, 1)


def _wkv(rw, ka, bv, s0_t, nb, n_steps, split, time_tile):
    nv = HEAD // split
    bg = LANES // (split * HEADS)
    op_spec = pl.BlockSpec((time_tile, HEADS, bg, LANES), lambda g, ti: (ti, 0, g, 0))
    y_spec = pl.BlockSpec((time_tile, HEADS, bg, HEAD), lambda g, ti: (ti, 0, g, 0))
    s_spec = pl.BlockSpec((nv, HEAD, LANES), lambda g, ti: (0, 0, g))
    return pl.pallas_call(
        functools.partial(_wkv_kernel, nv=nv, steps=time_tile, split=split),
        grid=(nb // bg, n_steps // time_tile),
        in_specs=[op_spec] * 3 + [s_spec],
        out_specs=[y_spec, s_spec],
        out_shape=[jax.ShapeDtypeStruct((n_steps, HEADS, nb, HEAD), F32), jax.ShapeDtypeStruct(s0_t.shape, F32)],
        scratch_shapes=[pltpu.VMEM((2, 3, LANES, LANES), F32), pltpu.VMEM((2, nv, LANES), F32)],
        compiler_params=pltpu.CompilerParams(dimension_semantics=("parallel", "arbitrary"),
                                             vmem_limit_bytes=V7X_VMEM_LIMIT),
        name="wkv_scan",
    )(rw, ka, bv, s0_t)


def _to_wkv_state(s, nb, split):
    bg = LANES // (split * HEADS)
    s = s.reshape(nb // bg, bg, HEADS, split, HEAD // split, HEAD).transpose(4, 5, 0, 3, 2, 1)
    return s.reshape(HEAD // split, HEAD, (nb // bg) * LANES)


def _from_wkv_state(s, nb, split):
    bg = LANES // (split * HEADS)
    s = s.reshape(HEAD // split, HEAD, nb // bg, split, HEADS, bg).transpose(2, 5, 4, 3, 0, 1)
    return s.reshape(nb, HEADS, HEAD, HEAD)


def _mix2_kernel(x_ref, g2_ref, ylru_ref, y_ref, bonus_ref, g_ref, lnw_ref, lnb_ref, ones_ref, wo_ref, o_ref):
    y = jnp.concatenate([y_ref[:, h].reshape(TOKEN_TILE, HEAD) for h in range(HEADS)], axis=1)
    mu = _segsum(y, ones_ref) * (1.0 / HEAD)
    d = y - mu
    var = _segsum(d * d, ones_ref) * (1.0 / HEAD)
    yn = d * lax.rsqrt(var + GN_EPS) * lnw_ref[...] + lnb_ref[...]
    yrw = ((yn + bonus_ref[...]) * g_ref[...]).astype(BF16)
    m = (jnp.dot(ylru_ref[...].astype(BF16), wo_ref[0:D_LRU, :], preferred_element_type=F32)
         + jnp.dot(yrw, wo_ref[D_LRU:D_LRU + D_RWKV, :], preferred_element_type=F32))
    o_ref[...] = x_ref[...] + g2_ref[...] * m


def _mix2(x, mod, ylru, y, bonus, g, lnw, lnb, ones_bd, w_out):
    nb = y.shape[2]
    y_spec = pl.BlockSpec((TOKEN_TILE // nb, HEADS, nb, HEAD), lambda i: (i, 0, 0, 0))
    return pl.pallas_call(
        _mix2_kernel,
        grid=(x.shape[0] // TOKEN_TILE,),
        in_specs=[_row_spec(D_MODEL), _mod_spec(5), _row_spec(D_LRU), y_spec, _row_spec(D_RWKV), _row_spec(D_RWKV),
                  _const_spec(lnw), _const_spec(lnb), _const_spec(ones_bd), _const_spec(w_out)],
        out_specs=_row_spec(D_MODEL),
        out_shape=jax.ShapeDtypeStruct(x.shape, F32),
        compiler_params=_TOKEN_PARAMS,
        name="mix_post",
    )(x, mod, ylru, y, bonus, g, lnw, lnb, ones_bd, w_out)


def _block_diag(w):
    n, c, d = w.shape
    eye = jnp.eye(n, dtype=w.dtype)
    return (eye[:, None, :, None] * w[:, :, None, :]).reshape(n * c, n * d)


def _row(v):
    return v.reshape(1, -1)


def _layer_params(P, l):
    z = jnp.zeros((R_DECAY, D_RWKV), F32)
    w2p = jnp.concatenate([P["rwkv_w2"][l], z], axis=0).astype(BF16)
    a2p = jnp.concatenate([z, P["rwkv_a2"][l]], axis=0).astype(BF16)
    ones_bd = _block_diag(jnp.ones((HEADS, HEAD, HEAD), F32)).astype(BF16)
    mix1 = [P["lru_conv_w"][l], _row(P["lru_conv_b"][l]),
            _block_diag(P["lru_wx"][l]).astype(BF16), _row(P["lru_bx"][l]),
            _block_diag(P["lru_wa"][l]).astype(BF16), _row(P["lru_ba"][l]), _row(P["lru_lambda"][l]),
            _row(P["rwkv_mu"][l]), _row(P["rwkv_w0"][l]), w2p, _row(P["rwkv_a0"][l]), a2p,
            P["rwkv_g2"][l].astype(BF16), _row(P["rwkv_k_k"][l]), _row(P["rwkv_k_a"][l]),
            _row(P["rwkv_r_k"][l]), ones_bd]
    return dict(
        ffn1=(_row(P["ffn1_norm"][l]), P["ffn1_w_gate"][l].astype(BF16), P["ffn1_w_up"][l].astype(BF16),
              P["ffn1_w_down"][l].astype(BF16)),
        ffn2=(_row(P["ffn2_norm"][l]), P["ffn2_w_gate"][l].astype(BF16), P["ffn2_w_up"][l].astype(BF16),
              P["ffn2_w_down"][l].astype(BF16)),
        mix_norm=_row(P["mix_norm"][l]), w_in=P["w_in"][l].astype(BF16), w_out=P["w_out"][l].astype(BF16),
        mix1=mix1, ln_w=_row(P["rwkv_ln_w"][l]), ln_b=_row(P["rwkv_ln_b"][l]), ones_bd=ones_bd)


def _trunk(x, mods, conv, hst, shift, wkv, layers, final_norm, nb, n_steps, split, time_tile):
    n_conv, n_h, n_shift, n_wkv = [], [], [], []
    for l, lp in enumerate(layers):
        mod = jnp.tile(mods[l], (TOKEN_TILE // nb, 1))
        x = _ffn(x, mod, 0, *lp["ffn1"])
        proj = _inproj(x, mod, lp["mix_norm"], lp["w_in"])
        conv0 = conv[l].transpose(1, 0, 2).reshape((CONV_W - 1) * nb, D_LRU)
        ylru, rw, ka, bv, g, bonus, convn, hn, shiftn = _mix1(proj, conv0, hst[l], shift[l], lp["mix1"], nb, n_steps)
        y, s_t = _wkv(rw, ka, bv, _to_wkv_state(wkv[l], nb, split), nb, n_steps, split, time_tile)
        x = _mix2(x, mod, ylru, y, bonus, g,
                  lp["ln_w"], lp["ln_b"], lp["ones_bd"], lp["w_out"])
        x = _ffn(x, mod, 6, *lp["ffn2"], final_norm=final_norm if l == DEPTH - 1 else None)
        n_conv.append(convn.reshape(CONV_W - 1, nb, D_LRU).transpose(1, 0, 2))
        n_h.append(hn)
        n_shift.append(shiftn)
        n_wkv.append(_from_wkv_state(s_t, nb, split))
    return x, (jnp.stack(n_conv), jnp.stack(n_h), jnp.stack(n_shift), jnp.stack(n_wkv))


def kernel(x_prompt, x_sample, c_prompt, c_sample, state_lru_conv, state_lru_h, state_rwkv_shift, state_rwkv_wkv, w_ada, b_ada, ffn1_norm, ffn1_w_gate, ffn1_w_up, ffn1_w_down, mix_norm, w_in, w_out, lru_conv_w, lru_conv_b, lru_wx, lru_bx, lru_wa, lru_ba, lru_lambda, rwkv_mu, rwkv_w0, rwkv_w2, rwkv_a0, rwkv_a2, rwkv_g2, rwkv_k_k, rwkv_k_a, rwkv_r_k, rwkv_ln_w, rwkv_ln_b, ffn2_norm, ffn2_w_gate, ffn2_w_up, ffn2_w_down, final_norm):
    P = dict(ffn1_norm=ffn1_norm, ffn1_w_gate=ffn1_w_gate, ffn1_w_up=ffn1_w_up, ffn1_w_down=ffn1_w_down,
             mix_norm=mix_norm, w_in=w_in, w_out=w_out, lru_conv_w=lru_conv_w, lru_conv_b=lru_conv_b,
             lru_wx=lru_wx, lru_bx=lru_bx, lru_wa=lru_wa, lru_ba=lru_ba, lru_lambda=lru_lambda,
             rwkv_mu=rwkv_mu, rwkv_w0=rwkv_w0, rwkv_w2=rwkv_w2, rwkv_a0=rwkv_a0, rwkv_a2=rwkv_a2,
             rwkv_g2=rwkv_g2, rwkv_k_k=rwkv_k_k, rwkv_k_a=rwkv_k_a, rwkv_r_k=rwkv_r_k,
             rwkv_ln_w=rwkv_ln_w, rwkv_ln_b=rwkv_ln_b, ffn2_norm=ffn2_norm, ffn2_w_gate=ffn2_w_gate,
             ffn2_w_up=ffn2_w_up, ffn2_w_down=ffn2_w_down)
    layers = [_layer_params(P, l) for l in range(DEPTH)]
    fn = _row(final_norm)
    bp, tp, _ = x_prompt.shape
    bs, ts, _ = x_sample.shape

    mod = _ada(jnp.concatenate([c_prompt, c_sample], axis=0), w_ada, b_ada)

    def time_major(x):
        b, t, d = x.shape
        return x.transpose(1, 0, 2).reshape(t * b, d)

    zeros = lambda *s: jnp.zeros((DEPTH,) + s, F32)
    y_p, st_p = _trunk(time_major(x_prompt), mod[:, :bp], zeros(bp, CONV_W - 1, D_LRU), zeros(bp, D_LRU),
                       zeros(bp, RWKV_PROJ_W), zeros(bp, HEADS, HEAD, HEAD), layers, fn,
                       bp, tp, LANES // (bp * HEADS), WKV_TIME_TILE)
    y_s, st_s = _trunk(time_major(x_sample), mod[:, bp:], state_lru_conv, state_lru_h, state_rwkv_shift,
                       state_rwkv_wkv, layers, fn, bs, ts, 1, ts)

    y_prompt = y_p.reshape(tp, bp, D_MODEL).transpose(1, 0, 2)
    y_sample = y_s.reshape(ts, bs, D_MODEL).transpose(1, 0, 2)
    return (y_prompt, y_sample) + st_p + st_s
```

```python
import functools

import jax
import jax.numpy as jnp
from jax import lax
from jax.experimental import pallas as pl
from jax.experimental.pallas import tpu as pltpu

F32 = jnp.float32
BF16 = jnp.bfloat16

D_MODEL = 1024
DEPTH = 2
D_LRU = 512
LRU_BLOCKS = 8
CONV_W = 4
LRU_C = 8.0
D_RWKV = 512
HEAD = 64
HEADS = D_RWKV // HEAD
R_DECAY = 64
R_ICLR = 64
R_GATE = 128
RWKV_PROJ_W = 3 * D_RWKV + R_DECAY + R_ICLR + R_GATE
IN_W = 2 * D_LRU + RWKV_PROJ_W
D_FF = 2816
N_MOD = 9
NORM_EPS = 1e-6
GN_EPS = 64e-5
L2_EPS = 1e-12

LANES = 128
SUBLANES = 8
V7X_VMEM_LIMIT = 56 * 1024 * 1024
MXU_TILE = 256
FF_CHUNKS = (768, 768, 768, 512)
assert sum(FF_CHUNKS) == D_FF and all(c % MXU_TILE == 0 for c in FF_CHUNKS)
TOKEN_TILE = 512
MIX_ROWS = 256
WKV_TIME_TILE = 64

assert 2 * HEAD == LANES


def _const_spec(arr):
    nd = arr.ndim
    return pl.BlockSpec(arr.shape, lambda *_: (0,) * nd, pipeline_mode=pl.Buffered(1))


def _softplus(z):
    return jnp.maximum(z, 0.0) + jnp.log1p(jnp.exp(-jnp.abs(z)))


def _segsum(q, ones_ref):
    hi = q.astype(BF16)
    lo = (q - hi.astype(F32)).astype(BF16)
    ones = ones_ref[...]
    return (jnp.dot(hi, ones, preferred_element_type=F32)
            + jnp.dot(lo, ones, preferred_element_type=F32))


def _rms_mod(x, nw, sc, sh):
    ms = jnp.mean(x * x, axis=-1, keepdims=True)
    h = x * lax.rsqrt(ms + NORM_EPS) * nw
    return h * (1.0 + sc) + sh


def _ada_kernel(c_ref, w_ref, b_ref, o_ref):
    c = c_ref[...]
    cs = (c * jax.nn.sigmoid(c)).astype(BF16)
    o_ref[...] = jnp.dot(cs, w_ref[...].astype(BF16), preferred_element_type=F32) + b_ref[...]


def _ada(c_all, w_ada, b_ada):
    n = c_all.shape[0]
    return pl.pallas_call(
        _ada_kernel,
        grid=(DEPTH, N_MOD),
        in_specs=[
            pl.BlockSpec((n, D_MODEL), lambda l, j: (0, 0)),
            pl.BlockSpec((None, D_MODEL, D_MODEL), lambda l, j: (l, 0, j)),
            pl.BlockSpec((None, 1, D_MODEL), lambda l, j: (l, 0, j)),
        ],
        out_specs=pl.BlockSpec((None, n, D_MODEL), lambda l, j: (l, 0, j)),
        out_shape=jax.ShapeDtypeStruct((DEPTH, n, N_MOD * D_MODEL), F32),
        compiler_params=pltpu.CompilerParams(dimension_semantics=("parallel", "parallel")),
        name="ada_mod",
    )(c_all, w_ada, b_ada.reshape(DEPTH, 1, N_MOD * D_MODEL))


def _row_spec(width):
    return pl.BlockSpec((TOKEN_TILE, width), lambda i: (i, 0))


def _mod_spec(col):
    return pl.BlockSpec((TOKEN_TILE, D_MODEL), lambda i: (0, col))


_TOKEN_PARAMS = pltpu.CompilerParams(dimension_semantics=("parallel",), vmem_limit_bytes=V7X_VMEM_LIMIT)


def _ffn_kernel(x_ref, sh_ref, sc_ref, g_ref, nw_ref, wg_ref, wu_ref, wd_ref, *rest, final, bm_in, bm_out):
    if final:
        fn_ref, o_ref = rest
    else:
        (o_ref,) = rest
    x = x_ref[...]
    if bm_in:
        x = jnp.swapaxes(x, 0, 1).reshape(TOKEN_TILE, D_MODEL)
    hb = _rms_mod(x, nw_ref[...], sc_ref[...], sh_ref[...]).astype(BF16)
    acc = jnp.zeros(x.shape, F32)
    hi = 0
    for width in FF_CHUNKS:
        lo, hi = hi, hi + width
        g = jnp.dot(hb, wg_ref[:, lo:hi], preferred_element_type=F32)
        u = jnp.dot(hb, wu_ref[:, lo:hi], preferred_element_type=F32)
        a = (g * jax.nn.sigmoid(g) * u).astype(BF16)
        acc = acc + jnp.dot(a, wd_ref[lo:hi, :], preferred_element_type=F32)
    y = x + 0.5 * g_ref[...] * acc
    if final:
        ms = jnp.mean(y * y, axis=-1, keepdims=True)
        y = y * lax.rsqrt(ms + NORM_EPS) * fn_ref[...]
    if bm_out:
        nb = o_ref.shape[0]
        y = jnp.swapaxes(y.reshape(TOKEN_TILE // nb, nb, D_MODEL), 0, 1)
    o_ref[...] = y


def _ffn(x, mod, first_col, nw, wg, wu, wd, final_norm=None, nb=None, bm_in=False, bm_out=False):
    final = final_norm is not None
    n_rows = x.shape[0] * x.shape[1] if bm_in else x.shape[0]
    bm_spec = None if nb is None else pl.BlockSpec((nb, TOKEN_TILE // nb, D_MODEL), lambda i: (0, i, 0))
    ins = [x, mod, mod, mod, nw, wg, wu, wd]
    specs = [bm_spec if bm_in else _row_spec(D_MODEL), _mod_spec(first_col), _mod_spec(first_col + 1),
             _mod_spec(first_col + 2), _const_spec(nw), _const_spec(wg), _const_spec(wu), _const_spec(wd)]
    if final:
        ins.append(final_norm)
        specs.append(_const_spec(final_norm))
    out_shape = (nb, n_rows // nb, D_MODEL) if bm_out else (n_rows, D_MODEL)
    return pl.pallas_call(
        functools.partial(_ffn_kernel, final=final, bm_in=bm_in, bm_out=bm_out),
        grid=(n_rows // TOKEN_TILE,),
        in_specs=specs,
        out_specs=bm_spec if bm_out else _row_spec(D_MODEL),
        out_shape=jax.ShapeDtypeStruct(out_shape, F32),
        compiler_params=_TOKEN_PARAMS,
        name="ffn_final" if final else "ffn",
    )(*ins)


def _inproj_kernel(x_ref, sh_ref, sc_ref, nw_ref, w_ref, o_ref):
    hb = _rms_mod(x_ref[...], nw_ref[...], sc_ref[...], sh_ref[...]).astype(BF16)
    o_ref[...] = jnp.dot(hb, w_ref[...], preferred_element_type=F32)


def _inproj(x, mod, nw, w_in):
    return pl.pallas_call(
        _inproj_kernel,
        grid=(x.shape[0] // TOKEN_TILE,),
        in_specs=[_row_spec(D_MODEL), _mod_spec(3), _mod_spec(4), _const_spec(nw), _const_spec(w_in)],
        out_specs=_row_spec(IN_W),
        out_shape=jax.ShapeDtypeStruct((x.shape[0], IN_W), F32),
        compiler_params=_TOKEN_PARAMS,
        name="mix_inproj",
    )(x, mod, mod, nw, w_in)


def _store_packed(ref, a, b):
    steps, _, nb, _ = ref.shape
    lane = lax.broadcasted_iota(jnp.int32, (a.shape[0], LANES), 1)
    first = lane < HEAD
    for hp in range(HEADS // 2):
        ap = a[:, hp * LANES:(hp + 1) * LANES]
        bp = b[:, hp * LANES:(hp + 1) * LANES]
        ref[:, 2 * hp] = jnp.where(first, ap, pltpu.roll(bp, HEAD, 1)).reshape(steps, nb, LANES)
        ref[:, 2 * hp + 1] = jnp.where(first, pltpu.roll(ap, HEAD, 1), bp).reshape(steps, nb, LANES)


def _mix1_kernel(p_ref, conv0_ref, h0_ref, shift0_ref,
                 cw_ref, cb_ref, wx_ref, bx_ref, wa_ref, ba_ref, lam_ref,
                 mu_ref, w0_ref, w2_ref, a0_ref, a2_ref, g2_ref, kk_ref, ka_ref, rk_ref, ones_ref,
                 ylru_ref, rw_ref, ka_out_ref, bv_ref, g_ref, bonus_ref,
                 convn_ref, hn_ref, shiftn_ref,
                 xe_ref, pe_ref, hc_ref, a_s, b_s, h_s, *, nb_rows, steps):
    B, R = nb_rows, nb_rows * steps
    CB = (CONV_W - 1) * B

    @pl.when(pl.program_id(0) == 0)
    def _():
        xe_ref[0:CB, :] = conv0_ref[...]
        pe_ref[0:B, :] = shift0_ref[...]
        hc_ref[...] = h0_ref[...]

    xe_ref[CB:CB + R, :] = p_ref[:, 0:D_LRU]
    gb = p_ref[:, D_LRU:2 * D_LRU]
    xc = cb_ref[...] + xe_ref[CB:CB + R, :] * cw_ref[CONV_W - 1:CONV_W, :]
    for j in range(CONV_W - 1):
        xc = xc + xe_ref[j * B:j * B + R, :] * cw_ref[j:j + 1, :]
    xcb = xc.astype(BF16)
    gate_x = jax.nn.sigmoid(jnp.dot(xcb, wx_ref[...], preferred_element_type=F32) + bx_ref[...])
    gate_a = jax.nn.sigmoid(jnp.dot(xcb, wa_ref[...], preferred_element_type=F32) + ba_ref[...])
    log_a = -LRU_C * gate_a * _softplus(-lam_ref[...])
    a = jnp.exp(log_a)
    a_s[...] = a
    b_s[...] = jnp.sqrt(1.0 - a * a) * gate_x * xc

    def scan_step(t, carry):
        sl = pl.ds(pl.multiple_of(t * B, B), B)
        h = a_s[sl, :] * hc_ref[...] + b_s[sl, :]
        hc_ref[...] = h
        h_s[sl, :] = h
        return carry

    lax.fori_loop(0, steps, scan_step, 0)
    ylru_ref[...] = h_s[...] * jax.nn.gelu(gb)
    tail = xe_ref[R:R + CB, :]
    xe_ref[0:CB, :] = tail
    convn_ref[...] = tail
    hn_ref[...] = hc_ref[...]

    pr = p_ref[:, 2 * D_LRU:IN_W]
    pe_ref[B:B + R, :] = pr
    xs = pr + (pe_ref[0:R, :] - pr) * mu_ref[...]
    last = pe_ref[R:R + B, :]
    pe_ref[0:B, :] = last
    shiftn_ref[...] = last

    r = xs[:, 0:D_RWKV]
    k = xs[:, D_RWKV:2 * D_RWKV]
    v = xs[:, 2 * D_RWKV:3 * D_RWKV]
    lowrank = xs[:, 3 * D_RWKV:3 * D_RWKV + R_DECAY + R_ICLR]
    gd = xs[:, 3 * D_RWKV + R_DECAY + R_ICLR:RWKV_PROJ_W]
    wlin = w0_ref[...] + jnp.dot(jnp.tanh(lowrank).astype(BF16), w2_ref[...], preferred_element_type=F32)
    w_log = -_softplus(-wlin) - 0.5
    a_rate = jax.nn.sigmoid(a0_ref[...] + jnp.dot(lowrank.astype(BF16), a2_ref[...], preferred_element_type=F32))
    kkr = k * kk_ref[...]
    kkn = kkr / jnp.maximum(jnp.sqrt(_segsum(kkr * kkr, ones_ref)), L2_EPS)
    kf = k * (1.0 + (a_rate - 1.0) * ka_ref[...])
    _store_packed(rw_ref, r, jnp.exp(-jnp.exp(w_log)))
    _store_packed(ka_out_ref, kf, -kkn)
    _store_packed(bv_ref, kkn * a_rate, v)
    g_ref[...] = jnp.dot(jax.nn.sigmoid(gd).astype(BF16), g2_ref[...], preferred_element_type=F32)
    bonus_ref[...] = _segsum(r * kf * rk_ref[...], ones_ref) * v


def _mix1(proj, conv0, h0, shift0, params, nb_rows, n_steps):
    n = proj.shape[0]
    steps = MIX_ROWS // nb_rows
    R = steps * nb_rows
    CB = (CONV_W - 1) * nb_rows
    row = lambda w: pl.BlockSpec((R, w), lambda i: (i, 0))
    packed = pl.BlockSpec((steps, HEADS, nb_rows, LANES), lambda i: (i, 0, 0, 0))
    full = lambda a: pl.BlockSpec(a.shape, lambda i: (0,) * a.ndim)
    wide = jax.ShapeDtypeStruct((n, D_RWKV), F32)
    pk = jax.ShapeDtypeStruct((n_steps, HEADS, nb_rows, LANES), F32)
    return pl.pallas_call(
        functools.partial(_mix1_kernel, nb_rows=nb_rows, steps=steps),
        grid=(n_steps // steps,),
        in_specs=[row(IN_W), full(conv0), full(h0), full(shift0)] + [_const_spec(p) for p in params],
        out_specs=[row(D_RWKV), packed, packed, packed, row(D_RWKV), row(D_RWKV),
                   full(conv0), full(h0), full(shift0)],
        out_shape=[wide, pk, pk, pk, wide, wide, jax.ShapeDtypeStruct(conv0.shape, F32),
                   jax.ShapeDtypeStruct(h0.shape, F32), jax.ShapeDtypeStruct(shift0.shape, F32)],
        scratch_shapes=[
            pltpu.VMEM((CB + R, D_LRU), F32),
            pltpu.VMEM((nb_rows + R, RWKV_PROJ_W), F32),
            pltpu.VMEM((nb_rows, D_LRU), F32),
            pltpu.VMEM((R, D_LRU), F32),
            pltpu.VMEM((R, D_LRU), F32),
            pltpu.VMEM((R, D_LRU), F32),
        ],
        compiler_params=pltpu.CompilerParams(dimension_semantics=("arbitrary",),
                                             vmem_limit_bytes=V7X_VMEM_LIMIT),
        name="mix_pre",
    )(proj, conv0, h0, shift0, *params)


def _wkv_kernel(rw_ref, ka_ref, bv_ref, s0_ref, y_ref, s_ref, ops_s, y_s, *, nv, steps, split):
    chains = LANES // split

    @pl.when(pl.program_id(1) == 0)
    def _():
        s_ref[...] = s0_ref[...]

    lane_split = lax.broadcasted_iota(jnp.int32, (nv, LANES), 1) // chains

    def stage_operands(t, slot):
        for j, ref in enumerate((rw_ref, ka_ref, bv_ref)):
            x = ref[t].reshape(chains, LANES)
            ops_s[slot, j] = jnp.concatenate([x] * split, axis=0).T

    def emit_y(t, slot):
        yt = y_s[slot].T
        yn = jnp.concatenate([yt[s * chains:(s + 1) * chains] for s in range(split)], axis=1)
        y_ref[t] = yn.reshape(HEADS, chains // HEADS, HEAD)

    def compute(slot):
        rt, wt = ops_s[slot, 0, 0:HEAD], ops_s[slot, 0, HEAD:2 * HEAD]
        kt, at = ops_s[slot, 1, 0:HEAD], ops_s[slot, 1, HEAD:2 * HEAD]
        bt, vall = ops_s[slot, 2, 0:HEAD], ops_s[slot, 2, HEAD:2 * HEAD]
        vt = vall[0:nv]
        for s in range(1, split):
            vt = jnp.where(lane_split == s, vall[s * nv:(s + 1) * nv], vt)
        rows = []
        for p in range(nv):
            st = s_ref[p]
            sa = jnp.sum(st * at, axis=0, keepdims=True)
            st = st * wt + sa * bt + vt[p:p + 1, :] * kt
            s_ref[p] = st
            rows.append(jnp.sum(st * rt, axis=0, keepdims=True))
        y_s[slot] = jnp.concatenate(rows, axis=0)

    stage_operands(0, 0)
    y_s[1] = jnp.zeros((nv, LANES), F32)

    def step_pair(i, carry):
        t0 = 2 * i
        stage_operands(t0 + 1, 1)
        compute(0)
        emit_y(jnp.maximum(t0 - 1, 0), 1)
        stage_operands(jnp.minimum(t0 + 2, steps - 1), 0)
        compute(1)
        emit_y(t0, 0)
        return carry

    lax.fori_loop(0, steps // 2, step_pair, 0)
    emit_y(steps - 1, 1)


def _wkv(rw, ka, bv, s0_t, nb, n_steps, split, time_tile):
    nv = HEAD // split
    bg = LANES // (split * HEADS)
    op_spec = pl.BlockSpec((time_tile, HEADS, bg, LANES), lambda g, ti: (ti, 0, g, 0))
    y_spec = pl.BlockSpec((time_tile, HEADS, bg, HEAD), lambda g, ti: (ti, 0, g, 0))
    s_spec = pl.BlockSpec((nv, HEAD, LANES), lambda g, ti: (0, 0, g))
    return pl.pallas_call(
        functools.partial(_wkv_kernel, nv=nv, steps=time_tile, split=split),
        grid=(nb // bg, n_steps // time_tile),
        in_specs=[op_spec] * 3 + [s_spec],
        out_specs=[y_spec, s_spec],
        out_shape=[jax.ShapeDtypeStruct((n_steps, HEADS, nb, HEAD), F32), jax.ShapeDtypeStruct(s0_t.shape, F32)],
        scratch_shapes=[pltpu.VMEM((2, 3, LANES, LANES), F32), pltpu.VMEM((2, nv, LANES), F32)],
        compiler_params=pltpu.CompilerParams(dimension_semantics=("parallel", "arbitrary"),
                                             vmem_limit_bytes=V7X_VMEM_LIMIT),
        name="wkv_scan",
    )(rw, ka, bv, s0_t)


def _to_wkv_state(s, nb, split):
    bg = LANES // (split * HEADS)
    s = s.reshape(nb // bg, bg, HEADS, split, HEAD // split, HEAD).transpose(4, 5, 0, 3, 2, 1)
    return s.reshape(HEAD // split, HEAD, (nb // bg) * LANES)


def _from_wkv_state(s, nb, split):
    bg = LANES // (split * HEADS)
    s = s.reshape(HEAD // split, HEAD, nb // bg, split, HEADS, bg).transpose(2, 5, 4, 3, 0, 1)
    return s.reshape(nb, HEADS, HEAD, HEAD)


def _mix2_kernel(x_ref, g2_ref, ylru_ref, y_ref, bonus_ref, g_ref, lnw_ref, lnb_ref, ones_ref, wo_ref, o_ref):
    y = jnp.concatenate([y_ref[:, h].reshape(TOKEN_TILE, HEAD) for h in range(HEADS)], axis=1)
    mu = _segsum(y, ones_ref) * (1.0 / HEAD)
    d = y - mu
    var = _segsum(d * d, ones_ref) * (1.0 / HEAD)
    yn = d * lax.rsqrt(var + GN_EPS) * lnw_ref[...] + lnb_ref[...]
    yrw = ((yn + bonus_ref[...]) * g_ref[...]).astype(BF16)
    m = (jnp.dot(ylru_ref[...].astype(BF16), wo_ref[0:D_LRU, :], preferred_element_type=F32)
         + jnp.dot(yrw, wo_ref[D_LRU:D_LRU + D_RWKV, :], preferred_element_type=F32))
    o_ref[...] = x_ref[...] + g2_ref[...] * m


def _mix2(x, mod, ylru, y, bonus, g, lnw, lnb, ones_bd, w_out):
    nb = y.shape[2]
    y_spec = pl.BlockSpec((TOKEN_TILE // nb, HEADS, nb, HEAD), lambda i: (i, 0, 0, 0))
    return pl.pallas_call(
        _mix2_kernel,
        grid=(x.shape[0] // TOKEN_TILE,),
        in_specs=[_row_spec(D_MODEL), _mod_spec(5), _row_spec(D_LRU), y_spec, _row_spec(D_RWKV), _row_spec(D_RWKV),
                  _const_spec(lnw), _const_spec(lnb), _const_spec(ones_bd), _const_spec(w_out)],
        out_specs=_row_spec(D_MODEL),
        out_shape=jax.ShapeDtypeStruct(x.shape, F32),
        compiler_params=_TOKEN_PARAMS,
        name="mix_post",
    )(x, mod, ylru, y, bonus, g, lnw, lnb, ones_bd, w_out)


def _block_diag(w):
    n, c, d = w.shape
    eye = jnp.eye(n, dtype=w.dtype)
    return (eye[:, None, :, None] * w[:, :, None, :]).reshape(n * c, n * d)


def _row(v):
    return v.reshape(1, -1)


def _layer_params(P, l):
    z = jnp.zeros((R_DECAY, D_RWKV), F32)
    w2p = jnp.concatenate([P["rwkv_w2"][l], z], axis=0).astype(BF16)
    a2p = jnp.concatenate([z, P["rwkv_a2"][l]], axis=0).astype(BF16)
    ones_bd = _block_diag(jnp.ones((HEADS, HEAD, HEAD), F32)).astype(BF16)
    mix1 = [P["lru_conv_w"][l], _row(P["lru_conv_b"][l]),
            _block_diag(P["lru_wx"][l]).astype(BF16), _row(P["lru_bx"][l]),
            _block_diag(P["lru_wa"][l]).astype(BF16), _row(P["lru_ba"][l]), _row(P["lru_lambda"][l]),
            _row(P["rwkv_mu"][l]), _row(P["rwkv_w0"][l]), w2p, _row(P["rwkv_a0"][l]), a2p,
            P["rwkv_g2"][l].astype(BF16), _row(P["rwkv_k_k"][l]), _row(P["rwkv_k_a"][l]),
            _row(P["rwkv_r_k"][l]), ones_bd]
    return dict(
        ffn1=(_row(P["ffn1_norm"][l]), P["ffn1_w_gate"][l].astype(BF16), P["ffn1_w_up"][l].astype(BF16),
              P["ffn1_w_down"][l].astype(BF16)),
        ffn2=(_row(P["ffn2_norm"][l]), P["ffn2_w_gate"][l].astype(BF16), P["ffn2_w_up"][l].astype(BF16),
              P["ffn2_w_down"][l].astype(BF16)),
        mix_norm=_row(P["mix_norm"][l]), w_in=P["w_in"][l].astype(BF16), w_out=P["w_out"][l].astype(BF16),
        mix1=mix1, ln_w=_row(P["rwkv_ln_w"][l]), ln_b=_row(P["rwkv_ln_b"][l]), ones_bd=ones_bd)


def _trunk(x, mods, conv, hst, shift, wkv, layers, final_norm, nb, n_steps, split, time_tile, batch_major_io):
    n_conv, n_h, n_shift, n_wkv = [], [], [], []
    for l, lp in enumerate(layers):
        mod = jnp.tile(mods[l], (TOKEN_TILE // nb, 1))
        x = _ffn(x, mod, 0, *lp["ffn1"], nb=nb, bm_in=batch_major_io and l == 0)
        proj = _inproj(x, mod, lp["mix_norm"], lp["w_in"])
        conv0 = conv[l].transpose(1, 0, 2).reshape((CONV_W - 1) * nb, D_LRU)
        ylru, rw, ka, bv, g, bonus, convn, hn, shiftn = _mix1(proj, conv0, hst[l], shift[l], lp["mix1"], nb, n_steps)
        y, s_t = _wkv(rw, ka, bv, _to_wkv_state(wkv[l], nb, split), nb, n_steps, split, time_tile)
        x = _mix2(x, mod, ylru, y, bonus, g,
                  lp["ln_w"], lp["ln_b"], lp["ones_bd"], lp["w_out"])
        last = l == DEPTH - 1
        x = _ffn(x, mod, 6, *lp["ffn2"], final_norm=final_norm if last else None, nb=nb,
                 bm_out=batch_major_io and last)
        n_conv.append(convn.reshape(CONV_W - 1, nb, D_LRU).transpose(1, 0, 2))
        n_h.append(hn)
        n_shift.append(shiftn)
        n_wkv.append(_from_wkv_state(s_t, nb, split))
    return x, (jnp.stack(n_conv), jnp.stack(n_h), jnp.stack(n_shift), jnp.stack(n_wkv))


def kernel(x_prompt, x_sample, c_prompt, c_sample, state_lru_conv, state_lru_h, state_rwkv_shift, state_rwkv_wkv, w_ada, b_ada, ffn1_norm, ffn1_w_gate, ffn1_w_up, ffn1_w_down, mix_norm, w_in, w_out, lru_conv_w, lru_conv_b, lru_wx, lru_bx, lru_wa, lru_ba, lru_lambda, rwkv_mu, rwkv_w0, rwkv_w2, rwkv_a0, rwkv_a2, rwkv_g2, rwkv_k_k, rwkv_k_a, rwkv_r_k, rwkv_ln_w, rwkv_ln_b, ffn2_norm, ffn2_w_gate, ffn2_w_up, ffn2_w_down, final_norm):
    P = dict(ffn1_norm=ffn1_norm, ffn1_w_gate=ffn1_w_gate, ffn1_w_up=ffn1_w_up, ffn1_w_down=ffn1_w_down,
             mix_norm=mix_norm, w_in=w_in, w_out=w_out, lru_conv_w=lru_conv_w, lru_conv_b=lru_conv_b,
             lru_wx=lru_wx, lru_bx=lru_bx, lru_wa=lru_wa, lru_ba=lru_ba, lru_lambda=lru_lambda,
             rwkv_mu=rwkv_mu, rwkv_w0=rwkv_w0, rwkv_w2=rwkv_w2, rwkv_a0=rwkv_a0, rwkv_a2=rwkv_a2,
             rwkv_g2=rwkv_g2, rwkv_k_k=rwkv_k_k, rwkv_k_a=rwkv_k_a, rwkv_r_k=rwkv_r_k,
             rwkv_ln_w=rwkv_ln_w, rwkv_ln_b=rwkv_ln_b, ffn2_norm=ffn2_norm, ffn2_w_gate=ffn2_w_gate,
             ffn2_w_up=ffn2_w_up, ffn2_w_down=ffn2_w_down)
    layers = [_layer_params(P, l) for l in range(DEPTH)]
    fn = _row(final_norm)
    bp, tp, _ = x_prompt.shape
    bs, ts, _ = x_sample.shape

    mod = _ada(jnp.concatenate([c_prompt, c_sample], axis=0), w_ada, b_ada)

    def time_major(x):
        b, t, d = x.shape
        return x.transpose(1, 0, 2).reshape(t * b, d)

    zeros = lambda *s: jnp.zeros((DEPTH,) + s, F32)
    y_prompt, st_p = _trunk(x_prompt, mod[:, :bp], zeros(bp, CONV_W - 1, D_LRU), zeros(bp, D_LRU),
                            zeros(bp, RWKV_PROJ_W), zeros(bp, HEADS, HEAD, HEAD), layers, fn,
                            bp, tp, LANES // (bp * HEADS), WKV_TIME_TILE, True)
    y_s, st_s = _trunk(time_major(x_sample), mod[:, bp:], state_lru_conv, state_lru_h, state_rwkv_shift,
                       state_rwkv_wkv, layers, fn, bs, ts, 1, ts, False)

    y_sample = y_s.reshape(ts, bs, D_MODEL).transpose(1, 0, 2)
    return (y_prompt, y_sample) + st_p + st_s
```

```python
import functools

import jax
import jax.numpy as jnp
from jax import lax
from jax.experimental import pallas as pl
from jax.experimental.pallas import tpu as pltpu

F32 = jnp.float32
BF16 = jnp.bfloat16

D_MODEL = 1024
DEPTH = 2
D_LRU = 512
LRU_BLOCKS = 8
CONV_W = 4
LRU_C = 8.0
D_RWKV = 512
HEAD = 64
HEADS = D_RWKV // HEAD
R_DECAY = 64
R_ICLR = 64
R_GATE = 128
RWKV_PROJ_W = 3 * D_RWKV + R_DECAY + R_ICLR + R_GATE
IN_W = 2 * D_LRU + RWKV_PROJ_W
D_FF = 2816
N_MOD = 9
NORM_EPS = 1e-6
GN_EPS = 64e-5
L2_EPS = 1e-12

LANES = 128
SUBLANES = 8
V7X_VMEM_LIMIT = 56 * 1024 * 1024
MXU_TILE = 256
FF_CHUNKS = (768, 768, 768, 512)
assert sum(FF_CHUNKS) == D_FF and all(c % MXU_TILE == 0 for c in FF_CHUNKS)
TOKEN_TILE = 512
MIX_ROWS = 256
WKV_TIME_TILE = 64
WKV_UNROLL = 4

assert 2 * HEAD == LANES


def _const_spec(arr):
    nd = arr.ndim
    return pl.BlockSpec(arr.shape, lambda *_: (0,) * nd, pipeline_mode=pl.Buffered(1))


def _softplus(z):
    return jnp.maximum(z, 0.0) + jnp.log1p(jnp.exp(-jnp.abs(z)))


def _segsum(q, ones_ref):
    hi = q.astype(BF16)
    lo = (q - hi.astype(F32)).astype(BF16)
    ones = ones_ref[...]
    return (jnp.dot(hi, ones, preferred_element_type=F32)
            + jnp.dot(lo, ones, preferred_element_type=F32))


def _rms_mod(x, nw, sc, sh):
    ms = jnp.mean(x * x, axis=-1, keepdims=True)
    h = x * lax.rsqrt(ms + NORM_EPS) * nw
    return h * (1.0 + sc) + sh


def _ada_kernel(c_ref, w_ref, b_ref, o_ref):
    c = c_ref[...]
    cs = (c * jax.nn.sigmoid(c)).astype(BF16)
    o_ref[...] = jnp.dot(cs, w_ref[...].astype(BF16), preferred_element_type=F32) + b_ref[...]


def _ada(c_all, w_ada, b_ada):
    n = c_all.shape[0]
    return pl.pallas_call(
        _ada_kernel,
        grid=(DEPTH, N_MOD),
        in_specs=[
            pl.BlockSpec((n, D_MODEL), lambda l, j: (0, 0)),
            pl.BlockSpec((None, D_MODEL, D_MODEL), lambda l, j: (l, 0, j)),
            pl.BlockSpec((None, 1, D_MODEL), lambda l, j: (l, 0, j)),
        ],
        out_specs=pl.BlockSpec((None, n, D_MODEL), lambda l, j: (l, 0, j)),
        out_shape=jax.ShapeDtypeStruct((DEPTH, n, N_MOD * D_MODEL), F32),
        compiler_params=pltpu.CompilerParams(dimension_semantics=("parallel", "parallel")),
        name="ada_mod",
    )(c_all, w_ada, b_ada.reshape(DEPTH, 1, N_MOD * D_MODEL))


def _row_spec(width):
    return pl.BlockSpec((TOKEN_TILE, width), lambda i: (i, 0))


def _mod_spec(col):
    return pl.BlockSpec((TOKEN_TILE, D_MODEL), lambda i: (0, col))


_TOKEN_PARAMS = pltpu.CompilerParams(dimension_semantics=("parallel",), vmem_limit_bytes=V7X_VMEM_LIMIT)


def _ffn_kernel(x_ref, sh_ref, sc_ref, g_ref, nw_ref, wg_ref, wu_ref, wd_ref, *rest, final, bm_in, bm_out):
    if final:
        fn_ref, o_ref = rest
    else:
        (o_ref,) = rest
    x = x_ref[...]
    if bm_in:
        x = jnp.swapaxes(x, 0, 1).reshape(TOKEN_TILE, D_MODEL)
    hb = _rms_mod(x, nw_ref[...], sc_ref[...], sh_ref[...]).astype(BF16)
    acc = jnp.zeros(x.shape, F32)
    hi = 0
    for width in FF_CHUNKS:
        lo, hi = hi, hi + width
        g = jnp.dot(hb, wg_ref[:, lo:hi], preferred_element_type=F32)
        u = jnp.dot(hb, wu_ref[:, lo:hi], preferred_element_type=F32)
        a = (g * jax.nn.sigmoid(g) * u).astype(BF16)
        acc = acc + jnp.dot(a, wd_ref[lo:hi, :], preferred_element_type=F32)
    y = x + 0.5 * g_ref[...] * acc
    if final:
        ms = jnp.mean(y * y, axis=-1, keepdims=True)
        y = y * lax.rsqrt(ms + NORM_EPS) * fn_ref[...]
    if bm_out:
        nb = o_ref.shape[0]
        y = jnp.swapaxes(y.reshape(TOKEN_TILE // nb, nb, D_MODEL), 0, 1)
    o_ref[...] = y


def _ffn(x, mod, first_col, nw, wg, wu, wd, final_norm=None, nb=None, bm_in=False, bm_out=False):
    final = final_norm is not None
    n_rows = x.shape[0] * x.shape[1] if bm_in else x.shape[0]
    bm_spec = None if nb is None else pl.BlockSpec((nb, TOKEN_TILE // nb, D_MODEL), lambda i: (0, i, 0))
    ins = [x, mod, mod, mod, nw, wg, wu, wd]
    specs = [bm_spec if bm_in else _row_spec(D_MODEL), _mod_spec(first_col), _mod_spec(first_col + 1),
             _mod_spec(first_col + 2), _const_spec(nw), _const_spec(wg), _const_spec(wu), _const_spec(wd)]
    if final:
        ins.append(final_norm)
        specs.append(_const_spec(final_norm))
    out_shape = (nb, n_rows // nb, D_MODEL) if bm_out else (n_rows, D_MODEL)
    return pl.pallas_call(
        functools.partial(_ffn_kernel, final=final, bm_in=bm_in, bm_out=bm_out),
        grid=(n_rows // TOKEN_TILE,),
        in_specs=specs,
        out_specs=bm_spec if bm_out else _row_spec(D_MODEL),
        out_shape=jax.ShapeDtypeStruct(out_shape, F32),
        compiler_params=_TOKEN_PARAMS,
        name="ffn_final" if final else "ffn",
    )(*ins)


def _inproj_kernel(x_ref, sh_ref, sc_ref, nw_ref, w_ref, o_ref):
    hb = _rms_mod(x_ref[...], nw_ref[...], sc_ref[...], sh_ref[...]).astype(BF16)
    o_ref[...] = jnp.dot(hb, w_ref[...], preferred_element_type=F32)


def _inproj(x, mod, nw, w_in):
    return pl.pallas_call(
        _inproj_kernel,
        grid=(x.shape[0] // TOKEN_TILE,),
        in_specs=[_row_spec(D_MODEL), _mod_spec(3), _mod_spec(4), _const_spec(nw), _const_spec(w_in)],
        out_specs=_row_spec(IN_W),
        out_shape=jax.ShapeDtypeStruct((x.shape[0], IN_W), F32),
        compiler_params=_TOKEN_PARAMS,
        name="mix_inproj",
    )(x, mod, mod, nw, w_in)


def _store_packed(ref, a, b):
    steps, _, nb, _ = ref.shape
    lane = lax.broadcasted_iota(jnp.int32, (a.shape[0], LANES), 1)
    first = lane < HEAD
    for hp in range(HEADS // 2):
        ap = a[:, hp * LANES:(hp + 1) * LANES]
        bp = b[:, hp * LANES:(hp + 1) * LANES]
        ref[:, 2 * hp] = jnp.where(first, ap, pltpu.roll(bp, HEAD, 1)).reshape(steps, nb, LANES)
        ref[:, 2 * hp + 1] = jnp.where(first, pltpu.roll(ap, HEAD, 1), bp).reshape(steps, nb, LANES)


def _mix1_kernel(p_ref, conv0_ref, h0_ref, shift0_ref,
                 cw_ref, cb_ref, wx_ref, bx_ref, wa_ref, ba_ref, lam_ref,
                 mu_ref, w0_ref, w2_ref, a0_ref, a2_ref, g2_ref, kk_ref, ka_ref, rk_ref, ones_ref,
                 ylru_ref, rw_ref, ka_out_ref, bv_ref, g_ref, bonus_ref,
                 convn_ref, hn_ref, shiftn_ref,
                 xe_ref, pe_ref, hc_ref, a_s, b_s, h_s, *, nb_rows, steps):
    B, R = nb_rows, nb_rows * steps
    CB = (CONV_W - 1) * B

    @pl.when(pl.program_id(0) == 0)
    def _():
        xe_ref[0:CB, :] = conv0_ref[...]
        pe_ref[0:B, :] = shift0_ref[...]
        hc_ref[...] = h0_ref[...]

    xe_ref[CB:CB + R, :] = p_ref[:, 0:D_LRU]
    gb = p_ref[:, D_LRU:2 * D_LRU]
    xc = cb_ref[...] + xe_ref[CB:CB + R, :] * cw_ref[CONV_W - 1:CONV_W, :]
    for j in range(CONV_W - 1):
        xc = xc + xe_ref[j * B:j * B + R, :] * cw_ref[j:j + 1, :]
    xcb = xc.astype(BF16)
    gate_x = jax.nn.sigmoid(jnp.dot(xcb, wx_ref[...], preferred_element_type=F32) + bx_ref[...])
    gate_a = jax.nn.sigmoid(jnp.dot(xcb, wa_ref[...], preferred_element_type=F32) + ba_ref[...])
    log_a = -LRU_C * gate_a * _softplus(-lam_ref[...])
    a = jnp.exp(log_a)
    a_s[...] = a
    b_s[...] = jnp.sqrt(1.0 - a * a) * gate_x * xc

    def scan_step(t, carry):
        sl = pl.ds(pl.multiple_of(t * B, B), B)
        h = a_s[sl, :] * hc_ref[...] + b_s[sl, :]
        hc_ref[...] = h
        h_s[sl, :] = h
        return carry

    lax.fori_loop(0, steps, scan_step, 0)
    ylru_ref[...] = h_s[...] * jax.nn.gelu(gb)
    tail = xe_ref[R:R + CB, :]
    xe_ref[0:CB, :] = tail
    convn_ref[...] = tail
    hn_ref[...] = hc_ref[...]

    pr = p_ref[:, 2 * D_LRU:IN_W]
    pe_ref[B:B + R, :] = pr
    xs = pr + (pe_ref[0:R, :] - pr) * mu_ref[...]
    last = pe_ref[R:R + B, :]
    pe_ref[0:B, :] = last
    shiftn_ref[...] = last

    r = xs[:, 0:D_RWKV]
    k = xs[:, D_RWKV:2 * D_RWKV]
    v = xs[:, 2 * D_RWKV:3 * D_RWKV]
    lowrank = xs[:, 3 * D_RWKV:3 * D_RWKV + R_DECAY + R_ICLR]
    gd = xs[:, 3 * D_RWKV + R_DECAY + R_ICLR:RWKV_PROJ_W]
    wlin = w0_ref[...] + jnp.dot(jnp.tanh(lowrank).astype(BF16), w2_ref[...], preferred_element_type=F32)
    w_log = -_softplus(-wlin) - 0.5
    a_rate = jax.nn.sigmoid(a0_ref[...] + jnp.dot(lowrank.astype(BF16), a2_ref[...], preferred_element_type=F32))
    kkr = k * kk_ref[...]
    kkn = kkr / jnp.maximum(jnp.sqrt(_segsum(kkr * kkr, ones_ref)), L2_EPS)
    kf = k * (1.0 + (a_rate - 1.0) * ka_ref[...])
    _store_packed(rw_ref, r, jnp.exp(-jnp.exp(w_log)))
    _store_packed(ka_out_ref, kf, -kkn)
    _store_packed(bv_ref, kkn * a_rate, v)
    g_ref[...] = jnp.dot(jax.nn.sigmoid(gd).astype(BF16), g2_ref[...], preferred_element_type=F32)
    bonus_ref[...] = _segsum(r * kf * rk_ref[...], ones_ref) * v


def _mix1(proj, conv0, h0, shift0, params, nb_rows, n_steps):
    n = proj.shape[0]
    steps = MIX_ROWS // nb_rows
    R = steps * nb_rows
    CB = (CONV_W - 1) * nb_rows
    row = lambda w: pl.BlockSpec((R, w), lambda i: (i, 0))
    packed = pl.BlockSpec((steps, HEADS, nb_rows, LANES), lambda i: (i, 0, 0, 0))
    full = lambda a: pl.BlockSpec(a.shape, lambda i: (0,) * a.ndim)
    wide = jax.ShapeDtypeStruct((n, D_RWKV), F32)
    pk = jax.ShapeDtypeStruct((n_steps, HEADS, nb_rows, LANES), F32)
    return pl.pallas_call(
        functools.partial(_mix1_kernel, nb_rows=nb_rows, steps=steps),
        grid=(n_steps // steps,),
        in_specs=[row(IN_W), full(conv0), full(h0), full(shift0)] + [_const_spec(p) for p in params],
        out_specs=[row(D_RWKV), packed, packed, packed, row(D_RWKV), row(D_RWKV),
                   full(conv0), full(h0), full(shift0)],
        out_shape=[wide, pk, pk, pk, wide, wide, jax.ShapeDtypeStruct(conv0.shape, F32),
                   jax.ShapeDtypeStruct(h0.shape, F32), jax.ShapeDtypeStruct(shift0.shape, F32)],
        scratch_shapes=[
            pltpu.VMEM((CB + R, D_LRU), F32),
            pltpu.VMEM((nb_rows + R, RWKV_PROJ_W), F32),
            pltpu.VMEM((nb_rows, D_LRU), F32),
            pltpu.VMEM((R, D_LRU), F32),
            pltpu.VMEM((R, D_LRU), F32),
            pltpu.VMEM((R, D_LRU), F32),
        ],
        compiler_params=pltpu.CompilerParams(dimension_semantics=("arbitrary",),
                                             vmem_limit_bytes=V7X_VMEM_LIMIT),
        name="mix_pre",
    )(proj, conv0, h0, shift0, *params)


def _wkv_kernel(rw_ref, ka_ref, bv_ref, s0_ref, y_ref, s_ref, ops_s, y_s, *, nv, steps, split):
    chains = LANES // split

    @pl.when(pl.program_id(1) == 0)
    def _():
        s_ref[...] = s0_ref[...]

    lane_split = lax.broadcasted_iota(jnp.int32, (nv, LANES), 1) // chains

    def stage_operands(t, slot):
        for j, ref in enumerate((rw_ref, ka_ref, bv_ref)):
            x = ref[t].reshape(chains, LANES)
            ops_s[slot, j] = jnp.concatenate([x] * split, axis=0).T

    def emit_y(t, slot):
        yt = y_s[slot].T
        yn = jnp.concatenate([yt[s * chains:(s + 1) * chains] for s in range(split)], axis=1)
        y_ref[t] = yn.reshape(HEADS, chains // HEADS, HEAD)

    def compute(slot):
        rt, wt = ops_s[slot, 0, 0:HEAD], ops_s[slot, 0, HEAD:2 * HEAD]
        kt, at = ops_s[slot, 1, 0:HEAD], ops_s[slot, 1, HEAD:2 * HEAD]
        bt, vall = ops_s[slot, 2, 0:HEAD], ops_s[slot, 2, HEAD:2 * HEAD]
        vt = vall[0:nv]
        for s in range(1, split):
            vt = jnp.where(lane_split == s, vall[s * nv:(s + 1) * nv], vt)
        rows = []
        for p in range(nv):
            st = s_ref[p]
            sa = jnp.sum(st * at, axis=0, keepdims=True)
            st = st * wt + sa * bt + vt[p:p + 1, :] * kt
            s_ref[p] = st
            rows.append(jnp.sum(st * rt, axis=0, keepdims=True))
        y_s[slot] = jnp.concatenate(rows, axis=0)

    stage_operands(0, 0)
    y_s[1] = jnp.zeros((nv, LANES), F32)

    def step_group(i, carry):
        for j in range(0, WKV_UNROLL, 2):
            t0 = WKV_UNROLL * i + j
            stage_operands(t0 + 1, 1)
            compute(0)
            emit_y(jnp.maximum(t0 - 1, 0), 1)
            stage_operands(jnp.minimum(t0 + 2, steps - 1), 0)
            compute(1)
            emit_y(t0, 0)
        return carry

    lax.fori_loop(0, steps // WKV_UNROLL, step_group, 0)
    emit_y(steps - 1, 1)


def _wkv(rw, ka, bv, s0_t, nb, n_steps, split, time_tile):
    nv = HEAD // split
    bg = LANES // (split * HEADS)
    op_spec = pl.BlockSpec((time_tile, HEADS, bg, LANES), lambda g, ti: (ti, 0, g, 0))
    y_spec = pl.BlockSpec((time_tile, HEADS, bg, HEAD), lambda g, ti: (ti, 0, g, 0))
    s_spec = pl.BlockSpec((nv, HEAD, LANES), lambda g, ti: (0, 0, g))
    return pl.pallas_call(
        functools.partial(_wkv_kernel, nv=nv, steps=time_tile, split=split),
        grid=(nb // bg, n_steps // time_tile),
        in_specs=[op_spec] * 3 + [s_spec],
        out_specs=[y_spec, s_spec],
        out_shape=[jax.ShapeDtypeStruct((n_steps, HEADS, nb, HEAD), F32), jax.ShapeDtypeStruct(s0_t.shape, F32)],
        scratch_shapes=[pltpu.VMEM((2, 3, LANES, LANES), F32), pltpu.VMEM((2, nv, LANES), F32)],
        compiler_params=pltpu.CompilerParams(dimension_semantics=("parallel", "arbitrary"),
                                             vmem_limit_bytes=V7X_VMEM_LIMIT),
        name="wkv_scan",
    )(rw, ka, bv, s0_t)


def _to_wkv_state(s, nb, split):
    bg = LANES // (split * HEADS)
    s = s.reshape(nb // bg, bg, HEADS, split, HEAD // split, HEAD).transpose(4, 5, 0, 3, 2, 1)
    return s.reshape(HEAD // split, HEAD, (nb // bg) * LANES)


def _from_wkv_state(s, nb, split):
    bg = LANES // (split * HEADS)
    s = s.reshape(HEAD // split, HEAD, nb // bg, split, HEADS, bg).transpose(2, 5, 4, 3, 0, 1)
    return s.reshape(nb, HEADS, HEAD, HEAD)


def _mix2_kernel(x_ref, g2_ref, ylru_ref, y_ref, bonus_ref, g_ref, lnw_ref, lnb_ref, ones_ref, wo_ref, o_ref):
    y = jnp.concatenate([y_ref[:, h].reshape(TOKEN_TILE, HEAD) for h in range(HEADS)], axis=1)
    mu = _segsum(y, ones_ref) * (1.0 / HEAD)
    d = y - mu
    var = _segsum(d * d, ones_ref) * (1.0 / HEAD)
    yn = d * lax.rsqrt(var + GN_EPS) * lnw_ref[...] + lnb_ref[...]
    yrw = ((yn + bonus_ref[...]) * g_ref[...]).astype(BF16)
    m = (jnp.dot(ylru_ref[...].astype(BF16), wo_ref[0:D_LRU, :], preferred_element_type=F32)
         + jnp.dot(yrw, wo_ref[D_LRU:D_LRU + D_RWKV, :], preferred_element_type=F32))
    o_ref[...] = x_ref[...] + g2_ref[...] * m


def _mix2(x, mod, ylru, y, bonus, g, lnw, lnb, ones_bd, w_out):
    nb = y.shape[2]
    y_spec = pl.BlockSpec((TOKEN_TILE // nb, HEADS, nb, HEAD), lambda i: (i, 0, 0, 0))
    return pl.pallas_call(
        _mix2_kernel,
        grid=(x.shape[0] // TOKEN_TILE,),
        in_specs=[_row_spec(D_MODEL), _mod_spec(5), _row_spec(D_LRU), y_spec, _row_spec(D_RWKV), _row_spec(D_RWKV),
                  _const_spec(lnw), _const_spec(lnb), _const_spec(ones_bd), _const_spec(w_out)],
        out_specs=_row_spec(D_MODEL),
        out_shape=jax.ShapeDtypeStruct(x.shape, F32),
        compiler_params=_TOKEN_PARAMS,
        name="mix_post",
    )(x, mod, ylru, y, bonus, g, lnw, lnb, ones_bd, w_out)


def _block_diag(w):
    n, c, d = w.shape
    eye = jnp.eye(n, dtype=w.dtype)
    return (eye[:, None, :, None] * w[:, :, None, :]).reshape(n * c, n * d)


def _row(v):
    return v.reshape(1, -1)


def _layer_params(P, l):
    z = jnp.zeros((R_DECAY, D_RWKV), F32)
    w2p = jnp.concatenate([P["rwkv_w2"][l], z], axis=0).astype(BF16)
    a2p = jnp.concatenate([z, P["rwkv_a2"][l]], axis=0).astype(BF16)
    ones_bd = _block_diag(jnp.ones((HEADS, HEAD, HEAD), F32)).astype(BF16)
    mix1 = [P["lru_conv_w"][l], _row(P["lru_conv_b"][l]),
            _block_diag(P["lru_wx"][l]).astype(BF16), _row(P["lru_bx"][l]),
            _block_diag(P["lru_wa"][l]).astype(BF16), _row(P["lru_ba"][l]), _row(P["lru_lambda"][l]),
            _row(P["rwkv_mu"][l]), _row(P["rwkv_w0"][l]), w2p, _row(P["rwkv_a0"][l]), a2p,
            P["rwkv_g2"][l].astype(BF16), _row(P["rwkv_k_k"][l]), _row(P["rwkv_k_a"][l]),
            _row(P["rwkv_r_k"][l]), ones_bd]
    return dict(
        ffn1=(_row(P["ffn1_norm"][l]), P["ffn1_w_gate"][l].astype(BF16), P["ffn1_w_up"][l].astype(BF16),
              P["ffn1_w_down"][l].astype(BF16)),
        ffn2=(_row(P["ffn2_norm"][l]), P["ffn2_w_gate"][l].astype(BF16), P["ffn2_w_up"][l].astype(BF16),
              P["ffn2_w_down"][l].astype(BF16)),
        mix_norm=_row(P["mix_norm"][l]), w_in=P["w_in"][l].astype(BF16), w_out=P["w_out"][l].astype(BF16),
        mix1=mix1, ln_w=_row(P["rwkv_ln_w"][l]), ln_b=_row(P["rwkv_ln_b"][l]), ones_bd=ones_bd)


def _trunk(x, mods, conv, hst, shift, wkv, layers, final_norm, nb, n_steps, split, time_tile, batch_major_io):
    n_conv, n_h, n_shift, n_wkv = [], [], [], []
    for l, lp in enumerate(layers):
        mod = jnp.tile(mods[l], (TOKEN_TILE // nb, 1))
        x = _ffn(x, mod, 0, *lp["ffn1"], nb=nb, bm_in=batch_major_io and l == 0)
        proj = _inproj(x, mod, lp["mix_norm"], lp["w_in"])
        conv0 = conv[l].transpose(1, 0, 2).reshape((CONV_W - 1) * nb, D_LRU)
        ylru, rw, ka, bv, g, bonus, convn, hn, shiftn = _mix1(proj, conv0, hst[l], shift[l], lp["mix1"], nb, n_steps)
        y, s_t = _wkv(rw, ka, bv, _to_wkv_state(wkv[l], nb, split), nb, n_steps, split, time_tile)
        x = _mix2(x, mod, ylru, y, bonus, g,
                  lp["ln_w"], lp["ln_b"], lp["ones_bd"], lp["w_out"])
        last = l == DEPTH - 1
        x = _ffn(x, mod, 6, *lp["ffn2"], final_norm=final_norm if last else None, nb=nb,
                 bm_out=batch_major_io and last)
        n_conv.append(convn.reshape(CONV_W - 1, nb, D_LRU).transpose(1, 0, 2))
        n_h.append(hn)
        n_shift.append(shiftn)
        n_wkv.append(_from_wkv_state(s_t, nb, split))
    return x, (jnp.stack(n_conv), jnp.stack(n_h), jnp.stack(n_shift), jnp.stack(n_wkv))


def kernel(x_prompt, x_sample, c_prompt, c_sample, state_lru_conv, state_lru_h, state_rwkv_shift, state_rwkv_wkv, w_ada, b_ada, ffn1_norm, ffn1_w_gate, ffn1_w_up, ffn1_w_down, mix_norm, w_in, w_out, lru_conv_w, lru_conv_b, lru_wx, lru_bx, lru_wa, lru_ba, lru_lambda, rwkv_mu, rwkv_w0, rwkv_w2, rwkv_a0, rwkv_a2, rwkv_g2, rwkv_k_k, rwkv_k_a, rwkv_r_k, rwkv_ln_w, rwkv_ln_b, ffn2_norm, ffn2_w_gate, ffn2_w_up, ffn2_w_down, final_norm):
    P = dict(ffn1_norm=ffn1_norm, ffn1_w_gate=ffn1_w_gate, ffn1_w_up=ffn1_w_up, ffn1_w_down=ffn1_w_down,
             mix_norm=mix_norm, w_in=w_in, w_out=w_out, lru_conv_w=lru_conv_w, lru_conv_b=lru_conv_b,
             lru_wx=lru_wx, lru_bx=lru_bx, lru_wa=lru_wa, lru_ba=lru_ba, lru_lambda=lru_lambda,
             rwkv_mu=rwkv_mu, rwkv_w0=rwkv_w0, rwkv_w2=rwkv_w2, rwkv_a0=rwkv_a0, rwkv_a2=rwkv_a2,
             rwkv_g2=rwkv_g2, rwkv_k_k=rwkv_k_k, rwkv_k_a=rwkv_k_a, rwkv_r_k=rwkv_r_k,
             rwkv_ln_w=rwkv_ln_w, rwkv_ln_b=rwkv_ln_b, ffn2_norm=ffn2_norm, ffn2_w_gate=ffn2_w_gate,
             ffn2_w_up=ffn2_w_up, ffn2_w_down=ffn2_w_down)
    layers = [_layer_params(P, l) for l in range(DEPTH)]
    fn = _row(final_norm)
    bp, tp, _ = x_prompt.shape
    bs, ts, _ = x_sample.shape

    mod = _ada(jnp.concatenate([c_prompt, c_sample], axis=0), w_ada, b_ada)

    def time_major(x):
        b, t, d = x.shape
        return x.transpose(1, 0, 2).reshape(t * b, d)

    zeros = lambda *s: jnp.zeros((DEPTH,) + s, F32)
    y_prompt, st_p = _trunk(x_prompt, mod[:, :bp], zeros(bp, CONV_W - 1, D_LRU), zeros(bp, D_LRU),
                            zeros(bp, RWKV_PROJ_W), zeros(bp, HEADS, HEAD, HEAD), layers, fn,
                            bp, tp, LANES // (bp * HEADS), WKV_TIME_TILE, True)
    y_s, st_s = _trunk(time_major(x_sample), mod[:, bp:], state_lru_conv, state_lru_h, state_rwkv_shift,
                       state_rwkv_wkv, layers, fn, bs, ts, 1, ts, False)

    y_sample = y_s.reshape(ts, bs, D_MODEL).transpose(1, 0, 2)
    return (y_prompt, y_sample) + st_p + st_s
```

```python
import functools

import jax
import jax.numpy as jnp
from jax import lax
from jax.experimental import pallas as pl
from jax.experimental.pallas import tpu as pltpu

F32 = jnp.float32
BF16 = jnp.bfloat16

D_MODEL = 1024
DEPTH = 2
D_LRU = 512
LRU_BLOCKS = 8
CONV_W = 4
LRU_C = 8.0
D_RWKV = 512
HEAD = 64
HEADS = D_RWKV // HEAD
R_DECAY = 64
R_ICLR = 64
R_GATE = 128
RWKV_PROJ_W = 3 * D_RWKV + R_DECAY + R_ICLR + R_GATE
IN_W = 2 * D_LRU + RWKV_PROJ_W
D_FF = 2816
N_MOD = 9
NORM_EPS = 1e-6
GN_EPS = 64e-5
L2_EPS = 1e-12

LANES = 128
SUBLANES = 8
V7X_VMEM_LIMIT = 56 * 1024 * 1024
MXU_TILE = 256
FF_CHUNKS = (768, 768, 768, 512)
assert sum(FF_CHUNKS) == D_FF and all(c % MXU_TILE == 0 for c in FF_CHUNKS)
TOKEN_TILE = 512
MIX_ROWS = 512
WKV_TIME_TILE = 64
WKV_UNROLL = 4

assert 2 * HEAD == LANES


def _const_spec(arr):
    nd = arr.ndim
    return pl.BlockSpec(arr.shape, lambda *_: (0,) * nd, pipeline_mode=pl.Buffered(1))


def _softplus(z):
    return jnp.maximum(z, 0.0) + jnp.log1p(jnp.exp(-jnp.abs(z)))


def _segsum(q, ones_ref):
    hi = q.astype(BF16)
    lo = (q - hi.astype(F32)).astype(BF16)
    ones = ones_ref[...]
    return (jnp.dot(hi, ones, preferred_element_type=F32)
            + jnp.dot(lo, ones, preferred_element_type=F32))


def _rms_mod(x, nw, sc, sh):
    ms = jnp.mean(x * x, axis=-1, keepdims=True)
    h = x * lax.rsqrt(ms + NORM_EPS) * nw
    return h * (1.0 + sc) + sh


def _ada_kernel(c_ref, w_ref, b_ref, o_ref):
    c = c_ref[...]
    cs = (c * jax.nn.sigmoid(c)).astype(BF16)
    o_ref[...] = jnp.dot(cs, w_ref[...].astype(BF16), preferred_element_type=F32) + b_ref[...]


def _ada(c_all, w_ada, b_ada):
    n = c_all.shape[0]
    return pl.pallas_call(
        _ada_kernel,
        grid=(DEPTH, N_MOD),
        in_specs=[
            pl.BlockSpec((n, D_MODEL), lambda l, j: (0, 0)),
            pl.BlockSpec((None, D_MODEL, D_MODEL), lambda l, j: (l, 0, j)),
            pl.BlockSpec((None, 1, D_MODEL), lambda l, j: (l, 0, j)),
        ],
        out_specs=pl.BlockSpec((None, n, D_MODEL), lambda l, j: (l, 0, j)),
        out_shape=jax.ShapeDtypeStruct((DEPTH, n, N_MOD * D_MODEL), F32),
        compiler_params=pltpu.CompilerParams(dimension_semantics=("parallel", "parallel")),
        name="ada_mod",
    )(c_all, w_ada, b_ada.reshape(DEPTH, 1, N_MOD * D_MODEL))


def _row_spec(width):
    return pl.BlockSpec((TOKEN_TILE, width), lambda i: (i, 0))


def _mod_spec(col):
    return pl.BlockSpec((TOKEN_TILE, D_MODEL), lambda i: (0, col))


_TOKEN_PARAMS = pltpu.CompilerParams(dimension_semantics=("parallel",), vmem_limit_bytes=V7X_VMEM_LIMIT)


def _ffn_kernel(x_ref, sh_ref, sc_ref, g_ref, nw_ref, wg_ref, wu_ref, wd_ref, *rest, final, bm_in, bm_out):
    if final:
        fn_ref, o_ref = rest
    else:
        (o_ref,) = rest
    x = x_ref[...]
    if bm_in:
        x = jnp.swapaxes(x, 0, 1).reshape(TOKEN_TILE, D_MODEL)
    hb = _rms_mod(x, nw_ref[...], sc_ref[...], sh_ref[...]).astype(BF16)
    acc = jnp.zeros(x.shape, F32)
    hi = 0
    for width in FF_CHUNKS:
        lo, hi = hi, hi + width
        g = jnp.dot(hb, wg_ref[:, lo:hi], preferred_element_type=F32)
        u = jnp.dot(hb, wu_ref[:, lo:hi], preferred_element_type=F32)
        a = (g * jax.nn.sigmoid(g) * u).astype(BF16)
        acc = acc + jnp.dot(a, wd_ref[lo:hi, :], preferred_element_type=F32)
    y = x + 0.5 * g_ref[...] * acc
    if final:
        ms = jnp.mean(y * y, axis=-1, keepdims=True)
        y = y * lax.rsqrt(ms + NORM_EPS) * fn_ref[...]
    if bm_out:
        nb = o_ref.shape[0]
        y = jnp.swapaxes(y.reshape(TOKEN_TILE // nb, nb, D_MODEL), 0, 1)
    o_ref[...] = y


def _ffn(x, mod, first_col, nw, wg, wu, wd, final_norm=None, nb=None, bm_in=False, bm_out=False):
    final = final_norm is not None
    n_rows = x.shape[0] * x.shape[1] if bm_in else x.shape[0]
    bm_spec = None if nb is None else pl.BlockSpec((nb, TOKEN_TILE // nb, D_MODEL), lambda i: (0, i, 0))
    ins = [x, mod, mod, mod, nw, wg, wu, wd]
    specs = [bm_spec if bm_in else _row_spec(D_MODEL), _mod_spec(first_col), _mod_spec(first_col + 1),
             _mod_spec(first_col + 2), _const_spec(nw), _const_spec(wg), _const_spec(wu), _const_spec(wd)]
    if final:
        ins.append(final_norm)
        specs.append(_const_spec(final_norm))
    out_shape = (nb, n_rows // nb, D_MODEL) if bm_out else (n_rows, D_MODEL)
    return pl.pallas_call(
        functools.partial(_ffn_kernel, final=final, bm_in=bm_in, bm_out=bm_out),
        grid=(n_rows // TOKEN_TILE,),
        in_specs=specs,
        out_specs=bm_spec if bm_out else _row_spec(D_MODEL),
        out_shape=jax.ShapeDtypeStruct(out_shape, F32),
        compiler_params=_TOKEN_PARAMS,
        name="ffn_final" if final else "ffn",
    )(*ins)


def _inproj_kernel(x_ref, sh_ref, sc_ref, nw_ref, w_ref, o_ref):
    hb = _rms_mod(x_ref[...], nw_ref[...], sc_ref[...], sh_ref[...]).astype(BF16)
    o_ref[...] = jnp.dot(hb, w_ref[...], preferred_element_type=F32)


def _inproj(x, mod, nw, w_in):
    return pl.pallas_call(
        _inproj_kernel,
        grid=(x.shape[0] // TOKEN_TILE,),
        in_specs=[_row_spec(D_MODEL), _mod_spec(3), _mod_spec(4), _const_spec(nw), _const_spec(w_in)],
        out_specs=_row_spec(IN_W),
        out_shape=jax.ShapeDtypeStruct((x.shape[0], IN_W), F32),
        compiler_params=_TOKEN_PARAMS,
        name="mix_inproj",
    )(x, mod, mod, nw, w_in)


def _store_packed(ref, a, b):
    steps, _, nb, _ = ref.shape
    lane = lax.broadcasted_iota(jnp.int32, (a.shape[0], LANES), 1)
    first = lane < HEAD
    for hp in range(HEADS // 2):
        ap = a[:, hp * LANES:(hp + 1) * LANES]
        bp = b[:, hp * LANES:(hp + 1) * LANES]
        ref[:, 2 * hp] = jnp.where(first, ap, pltpu.roll(bp, HEAD, 1)).reshape(steps, nb, LANES)
        ref[:, 2 * hp + 1] = jnp.where(first, pltpu.roll(ap, HEAD, 1), bp).reshape(steps, nb, LANES)


def _mix1_kernel(p_ref, conv0_ref, h0_ref, shift0_ref,
                 cw_ref, cb_ref, wx_ref, bx_ref, wa_ref, ba_ref, lam_ref,
                 mu_ref, w0_ref, w2_ref, a0_ref, a2_ref, g2_ref, kk_ref, ka_ref, rk_ref, ones_ref,
                 ylru_ref, rw_ref, ka_out_ref, bv_ref, g_ref, bonus_ref,
                 convn_ref, hn_ref, shiftn_ref,
                 xe_ref, pe_ref, hc_ref, a_s, b_s, h_s, *, nb_rows, steps):
    B, R = nb_rows, nb_rows * steps
    CB = (CONV_W - 1) * B

    @pl.when(pl.program_id(0) == 0)
    def _():
        xe_ref[0:CB, :] = conv0_ref[...]
        pe_ref[0:B, :] = shift0_ref[...]
        hc_ref[...] = h0_ref[...]

    xe_ref[CB:CB + R, :] = p_ref[:, 0:D_LRU]
    gb = p_ref[:, D_LRU:2 * D_LRU]
    xc = cb_ref[...] + xe_ref[CB:CB + R, :] * cw_ref[CONV_W - 1:CONV_W, :]
    for j in range(CONV_W - 1):
        xc = xc + xe_ref[j * B:j * B + R, :] * cw_ref[j:j + 1, :]
    xcb = xc.astype(BF16)
    gate_x = jax.nn.sigmoid(jnp.dot(xcb, wx_ref[...], preferred_element_type=F32) + bx_ref[...])
    gate_a = jax.nn.sigmoid(jnp.dot(xcb, wa_ref[...], preferred_element_type=F32) + ba_ref[...])
    log_a = -LRU_C * gate_a * _softplus(-lam_ref[...])
    a = jnp.exp(log_a)
    a_s[...] = a
    b_s[...] = jnp.sqrt(1.0 - a * a) * gate_x * xc

    def scan_step(t, carry):
        sl = pl.ds(pl.multiple_of(t * B, B), B)
        h = a_s[sl, :] * hc_ref[...] + b_s[sl, :]
        hc_ref[...] = h
        h_s[sl, :] = h
        return carry

    lax.fori_loop(0, steps, scan_step, 0)
    ylru_ref[...] = h_s[...] * jax.nn.gelu(gb)
    tail = xe_ref[R:R + CB, :]
    xe_ref[0:CB, :] = tail
    convn_ref[...] = tail
    hn_ref[...] = hc_ref[...]

    pr = p_ref[:, 2 * D_LRU:IN_W]
    pe_ref[B:B + R, :] = pr
    xs = pr + (pe_ref[0:R, :] - pr) * mu_ref[...]
    last = pe_ref[R:R + B, :]
    pe_ref[0:B, :] = last
    shiftn_ref[...] = last

    r = xs[:, 0:D_RWKV]
    k = xs[:, D_RWKV:2 * D_RWKV]
    v = xs[:, 2 * D_RWKV:3 * D_RWKV]
    lowrank = xs[:, 3 * D_RWKV:3 * D_RWKV + R_DECAY + R_ICLR]
    gd = xs[:, 3 * D_RWKV + R_DECAY + R_ICLR:RWKV_PROJ_W]
    wlin = w0_ref[...] + jnp.dot(jnp.tanh(lowrank).astype(BF16), w2_ref[...], preferred_element_type=F32)
    w_log = -_softplus(-wlin) - 0.5
    a_rate = jax.nn.sigmoid(a0_ref[...] + jnp.dot(lowrank.astype(BF16), a2_ref[...], preferred_element_type=F32))
    kkr = k * kk_ref[...]
    kkn = kkr / jnp.maximum(jnp.sqrt(_segsum(kkr * kkr, ones_ref)), L2_EPS)
    kf = k * (1.0 + (a_rate - 1.0) * ka_ref[...])
    _store_packed(rw_ref, r, jnp.exp(-jnp.exp(w_log)))
    _store_packed(ka_out_ref, kf, -kkn)
    _store_packed(bv_ref, kkn * a_rate, v)
    g_ref[...] = jnp.dot(jax.nn.sigmoid(gd).astype(BF16), g2_ref[...], preferred_element_type=F32)
    bonus_ref[...] = _segsum(r * kf * rk_ref[...], ones_ref) * v


def _mix1(proj, conv0, h0, shift0, params, nb_rows, n_steps):
    n = proj.shape[0]
    steps = MIX_ROWS // nb_rows
    R = steps * nb_rows
    CB = (CONV_W - 1) * nb_rows
    row = lambda w: pl.BlockSpec((R, w), lambda i: (i, 0))
    packed = pl.BlockSpec((steps, HEADS, nb_rows, LANES), lambda i: (i, 0, 0, 0))
    full = lambda a: pl.BlockSpec(a.shape, lambda i: (0,) * a.ndim)
    wide = jax.ShapeDtypeStruct((n, D_RWKV), F32)
    pk = jax.ShapeDtypeStruct((n_steps, HEADS, nb_rows, LANES), F32)
    return pl.pallas_call(
        functools.partial(_mix1_kernel, nb_rows=nb_rows, steps=steps),
        grid=(n_steps // steps,),
        in_specs=[row(IN_W), full(conv0), full(h0), full(shift0)] + [_const_spec(p) for p in params],
        out_specs=[row(D_RWKV), packed, packed, packed, row(D_RWKV), row(D_RWKV),
                   full(conv0), full(h0), full(shift0)],
        out_shape=[wide, pk, pk, pk, wide, wide, jax.ShapeDtypeStruct(conv0.shape, F32),
                   jax.ShapeDtypeStruct(h0.shape, F32), jax.ShapeDtypeStruct(shift0.shape, F32)],
        scratch_shapes=[
            pltpu.VMEM((CB + R, D_LRU), F32),
            pltpu.VMEM((nb_rows + R, RWKV_PROJ_W), F32),
            pltpu.VMEM((nb_rows, D_LRU), F32),
            pltpu.VMEM((R, D_LRU), F32),
            pltpu.VMEM((R, D_LRU), F32),
            pltpu.VMEM((R, D_LRU), F32),
        ],
        compiler_params=pltpu.CompilerParams(dimension_semantics=("arbitrary",),
                                             vmem_limit_bytes=V7X_VMEM_LIMIT),
        name="mix_pre",
    )(proj, conv0, h0, shift0, *params)


def _wkv_kernel(rw_ref, ka_ref, bv_ref, s0_ref, y_ref, s_ref, ops_s, y_s, *, nv, steps, split):
    chains = LANES // split

    @pl.when(pl.program_id(1) == 0)
    def _():
        s_ref[...] = s0_ref[...]

    lane_split = lax.broadcasted_iota(jnp.int32, (nv, LANES), 1) // chains

    def stage_operands(t, slot):
        for j, ref in enumerate((rw_ref, ka_ref, bv_ref)):
            x = ref[t].reshape(chains, LANES)
            ops_s[slot, j] = jnp.concatenate([x] * split, axis=0).T

    def emit_y(t, slot):
        yt = y_s[slot].T
        yn = jnp.concatenate([yt[s * chains:(s + 1) * chains] for s in range(split)], axis=1)
        y_ref[t] = yn.reshape(HEADS, chains // HEADS, HEAD)

    def compute(slot):
        rt, wt = ops_s[slot, 0, 0:HEAD], ops_s[slot, 0, HEAD:2 * HEAD]
        kt, at = ops_s[slot, 1, 0:HEAD], ops_s[slot, 1, HEAD:2 * HEAD]
        bt, vall = ops_s[slot, 2, 0:HEAD], ops_s[slot, 2, HEAD:2 * HEAD]
        vt = vall[0:nv]
        for s in range(1, split):
            vt = jnp.where(lane_split == s, vall[s * nv:(s + 1) * nv], vt)
        rows = []
        for p in range(nv):
            st = s_ref[p]
            sa = jnp.sum(st * at, axis=0, keepdims=True)
            st = st * wt + sa * bt + vt[p:p + 1, :] * kt
            s_ref[p] = st
            rows.append(jnp.sum(st * rt, axis=0, keepdims=True))
        y_s[slot] = jnp.concatenate(rows, axis=0)

    stage_operands(0, 0)
    y_s[1] = jnp.zeros((nv, LANES), F32)

    def step_group(i, carry):
        for j in range(0, WKV_UNROLL, 2):
            t0 = WKV_UNROLL * i + j
            stage_operands(t0 + 1, 1)
            compute(0)
            emit_y(jnp.maximum(t0 - 1, 0), 1)
            stage_operands(jnp.minimum(t0 + 2, steps - 1), 0)
            compute(1)
            emit_y(t0, 0)
        return carry

    lax.fori_loop(0, steps // WKV_UNROLL, step_group, 0)
    emit_y(steps - 1, 1)


def _wkv(rw, ka, bv, s0_t, nb, n_steps, split, time_tile):
    nv = HEAD // split
    bg = LANES // (split * HEADS)
    op_spec = pl.BlockSpec((time_tile, HEADS, bg, LANES), lambda g, ti: (ti, 0, g, 0))
    y_spec = pl.BlockSpec((time_tile, HEADS, bg, HEAD), lambda g, ti: (ti, 0, g, 0))
    s_spec = pl.BlockSpec((nv, HEAD, LANES), lambda g, ti: (0, 0, g))
    return pl.pallas_call(
        functools.partial(_wkv_kernel, nv=nv, steps=time_tile, split=split),
        grid=(nb // bg, n_steps // time_tile),
        in_specs=[op_spec] * 3 + [s_spec],
        out_specs=[y_spec, s_spec],
        out_shape=[jax.ShapeDtypeStruct((n_steps, HEADS, nb, HEAD), F32), jax.ShapeDtypeStruct(s0_t.shape, F32)],
        scratch_shapes=[pltpu.VMEM((2, 3, LANES, LANES), F32), pltpu.VMEM((2, nv, LANES), F32)],
        compiler_params=pltpu.CompilerParams(dimension_semantics=("parallel", "arbitrary"),
                                             vmem_limit_bytes=V7X_VMEM_LIMIT),
        name="wkv_scan",
    )(rw, ka, bv, s0_t)


def _to_wkv_state(s, nb, split):
    bg = LANES // (split * HEADS)
    s = s.reshape(nb // bg, bg, HEADS, split, HEAD // split, HEAD).transpose(4, 5, 0, 3, 2, 1)
    return s.reshape(HEAD // split, HEAD, (nb // bg) * LANES)


def _from_wkv_state(s, nb, split):
    bg = LANES // (split * HEADS)
    s = s.reshape(HEAD // split, HEAD, nb // bg, split, HEADS, bg).transpose(2, 5, 4, 3, 0, 1)
    return s.reshape(nb, HEADS, HEAD, HEAD)


def _mix2_kernel(x_ref, g2_ref, ylru_ref, y_ref, bonus_ref, g_ref, lnw_ref, lnb_ref, ones_ref, wo_ref, o_ref):
    y = jnp.concatenate([y_ref[:, h].reshape(TOKEN_TILE, HEAD) for h in range(HEADS)], axis=1)
    mu = _segsum(y, ones_ref) * (1.0 / HEAD)
    d = y - mu
    var = _segsum(d * d, ones_ref) * (1.0 / HEAD)
    yn = d * lax.rsqrt(var + GN_EPS) * lnw_ref[...] + lnb_ref[...]
    yrw = ((yn + bonus_ref[...]) * g_ref[...]).astype(BF16)
    m = (jnp.dot(ylru_ref[...].astype(BF16), wo_ref[0:D_LRU, :], preferred_element_type=F32)
         + jnp.dot(yrw, wo_ref[D_LRU:D_LRU + D_RWKV, :], preferred_element_type=F32))
    o_ref[...] = x_ref[...] + g2_ref[...] * m


def _mix2(x, mod, ylru, y, bonus, g, lnw, lnb, ones_bd, w_out):
    nb = y.shape[2]
    y_spec = pl.BlockSpec((TOKEN_TILE // nb, HEADS, nb, HEAD), lambda i: (i, 0, 0, 0))
    return pl.pallas_call(
        _mix2_kernel,
        grid=(x.shape[0] // TOKEN_TILE,),
        in_specs=[_row_spec(D_MODEL), _mod_spec(5), _row_spec(D_LRU), y_spec, _row_spec(D_RWKV), _row_spec(D_RWKV),
                  _const_spec(lnw), _const_spec(lnb), _const_spec(ones_bd), _const_spec(w_out)],
        out_specs=_row_spec(D_MODEL),
        out_shape=jax.ShapeDtypeStruct(x.shape, F32),
        compiler_params=_TOKEN_PARAMS,
        name="mix_post",
    )(x, mod, ylru, y, bonus, g, lnw, lnb, ones_bd, w_out)


def _block_diag(w):
    n, c, d = w.shape
    eye = jnp.eye(n, dtype=w.dtype)
    return (eye[:, None, :, None] * w[:, :, None, :]).reshape(n * c, n * d)


def _row(v):
    return v.reshape(1, -1)


def _layer_params(P, l):
    z = jnp.zeros((R_DECAY, D_RWKV), F32)
    w2p = jnp.concatenate([P["rwkv_w2"][l], z], axis=0).astype(BF16)
    a2p = jnp.concatenate([z, P["rwkv_a2"][l]], axis=0).astype(BF16)
    ones_bd = _block_diag(jnp.ones((HEADS, HEAD, HEAD), F32)).astype(BF16)
    mix1 = [P["lru_conv_w"][l], _row(P["lru_conv_b"][l]),
            _block_diag(P["lru_wx"][l]).astype(BF16), _row(P["lru_bx"][l]),
            _block_diag(P["lru_wa"][l]).astype(BF16), _row(P["lru_ba"][l]), _row(P["lru_lambda"][l]),
            _row(P["rwkv_mu"][l]), _row(P["rwkv_w0"][l]), w2p, _row(P["rwkv_a0"][l]), a2p,
            P["rwkv_g2"][l].astype(BF16), _row(P["rwkv_k_k"][l]), _row(P["rwkv_k_a"][l]),
            _row(P["rwkv_r_k"][l]), ones_bd]
    return dict(
        ffn1=(_row(P["ffn1_norm"][l]), P["ffn1_w_gate"][l].astype(BF16), P["ffn1_w_up"][l].astype(BF16),
              P["ffn1_w_down"][l].astype(BF16)),
        ffn2=(_row(P["ffn2_norm"][l]), P["ffn2_w_gate"][l].astype(BF16), P["ffn2_w_up"][l].astype(BF16),
              P["ffn2_w_down"][l].astype(BF16)),
        mix_norm=_row(P["mix_norm"][l]), w_in=P["w_in"][l].astype(BF16), w_out=P["w_out"][l].astype(BF16),
        mix1=mix1, ln_w=_row(P["rwkv_ln_w"][l]), ln_b=_row(P["rwkv_ln_b"][l]), ones_bd=ones_bd)


def _trunk(x, mods, conv, hst, shift, wkv, layers, final_norm, nb, n_steps, split, time_tile, batch_major_io):
    n_conv, n_h, n_shift, n_wkv = [], [], [], []
    for l, lp in enumerate(layers):
        mod = jnp.tile(mods[l], (TOKEN_TILE // nb, 1))
        x = _ffn(x, mod, 0, *lp["ffn1"], nb=nb, bm_in=batch_major_io and l == 0)
        proj = _inproj(x, mod, lp["mix_norm"], lp["w_in"])
        conv0 = conv[l].transpose(1, 0, 2).reshape((CONV_W - 1) * nb, D_LRU)
        ylru, rw, ka, bv, g, bonus, convn, hn, shiftn = _mix1(proj, conv0, hst[l], shift[l], lp["mix1"], nb, n_steps)
        y, s_t = _wkv(rw, ka, bv, _to_wkv_state(wkv[l], nb, split), nb, n_steps, split, time_tile)
        x = _mix2(x, mod, ylru, y, bonus, g,
                  lp["ln_w"], lp["ln_b"], lp["ones_bd"], lp["w_out"])
        last = l == DEPTH - 1
        x = _ffn(x, mod, 6, *lp["ffn2"], final_norm=final_norm if last else None, nb=nb,
                 bm_out=batch_major_io and last)
        n_conv.append(convn.reshape(CONV_W - 1, nb, D_LRU).transpose(1, 0, 2))
        n_h.append(hn)
        n_shift.append(shiftn)
        n_wkv.append(_from_wkv_state(s_t, nb, split))
    return x, (jnp.stack(n_conv), jnp.stack(n_h), jnp.stack(n_shift), jnp.stack(n_wkv))


def kernel(x_prompt, x_sample, c_prompt, c_sample, state_lru_conv, state_lru_h, state_rwkv_shift, state_rwkv_wkv, w_ada, b_ada, ffn1_norm, ffn1_w_gate, ffn1_w_up, ffn1_w_down, mix_norm, w_in, w_out, lru_conv_w, lru_conv_b, lru_wx, lru_bx, lru_wa, lru_ba, lru_lambda, rwkv_mu, rwkv_w0, rwkv_w2, rwkv_a0, rwkv_a2, rwkv_g2, rwkv_k_k, rwkv_k_a, rwkv_r_k, rwkv_ln_w, rwkv_ln_b, ffn2_norm, ffn2_w_gate, ffn2_w_up, ffn2_w_down, final_norm):
    P = dict(ffn1_norm=ffn1_norm, ffn1_w_gate=ffn1_w_gate, ffn1_w_up=ffn1_w_up, ffn1_w_down=ffn1_w_down,
             mix_norm=mix_norm, w_in=w_in, w_out=w_out, lru_conv_w=lru_conv_w, lru_conv_b=lru_conv_b,
             lru_wx=lru_wx, lru_bx=lru_bx, lru_wa=lru_wa, lru_ba=lru_ba, lru_lambda=lru_lambda,
             rwkv_mu=rwkv_mu, rwkv_w0=rwkv_w0, rwkv_w2=rwkv_w2, rwkv_a0=rwkv_a0, rwkv_a2=rwkv_a2,
             rwkv_g2=rwkv_g2, rwkv_k_k=rwkv_k_k, rwkv_k_a=rwkv_k_a, rwkv_r_k=rwkv_r_k,
             rwkv_ln_w=rwkv_ln_w, rwkv_ln_b=rwkv_ln_b, ffn2_norm=ffn2_norm, ffn2_w_gate=ffn2_w_gate,
             ffn2_w_up=ffn2_w_up, ffn2_w_down=ffn2_w_down)
    layers = [_layer_params(P, l) for l in range(DEPTH)]
    fn = _row(final_norm)
    bp, tp, _ = x_prompt.shape
    bs, ts, _ = x_sample.shape

    mod = _ada(jnp.concatenate([c_prompt, c_sample], axis=0), w_ada, b_ada)

    def time_major(x):
        b, t, d = x.shape
        return x.transpose(1, 0, 2).reshape(t * b, d)

    zeros = lambda *s: jnp.zeros((DEPTH,) + s, F32)
    y_prompt, st_p = _trunk(x_prompt, mod[:, :bp], zeros(bp, CONV_W - 1, D_LRU), zeros(bp, D_LRU),
                            zeros(bp, RWKV_PROJ_W), zeros(bp, HEADS, HEAD, HEAD), layers, fn,
                            bp, tp, LANES // (bp * HEADS), WKV_TIME_TILE, True)
    y_s, st_s = _trunk(time_major(x_sample), mod[:, bp:], state_lru_conv, state_lru_h, state_rwkv_shift,
                       state_rwkv_wkv, layers, fn, bs, ts, 1, ts, False)

    y_sample = y_s.reshape(ts, bs, D_MODEL).transpose(1, 0, 2)
    return (y_prompt, y_sample) + st_p + st_s
```
